```python
import jax, jax.numpy as jnp
from jax import lax
import numpy as np

D_MODEL = 2048
BATCH = 8
SEQ = 4096
DEPTH = 4

N_HEADS_A = 8
HEAD_DIM_A = 128
WIDTH_A = N_HEADS_A * HEAD_DIM_A
DILATED_PAIRS = ((128, 1), (512, 4), (2048, 16))
ATTN_BLOCK = 128
N_GROUPS_B = 8
GROUP_DIM_B = 128
WIDTH_B = N_GROUPS_B * GROUP_DIM_B
CHUNK_B = 128
AB_IN = 3 * WIDTH_A + 2 * WIDTH_B
AB_OUT = WIDTH_A + WIDTH_B
CONV_WIDTH = 31
D_FF = ((8 * D_MODEL // 3 + 255) // 256) * 256
N_EVEN = (DEPTH + 1) // 2
N_ODD = DEPTH // 2
EPS = 1e-6

kernel_name = "hybrid_dilated_gmlp_conformer_trunk"


def rms_norm(x, g):
    xf = x.astype(jnp.float32)
    y = xf * lax.rsqrt(jnp.mean(xf * xf, axis=-1, keepdims=True) + EPS)
    return (y * g.astype(jnp.float32)).astype(x.dtype)


def layer_norm(x, g, b):
    xf = x.astype(jnp.float32)
    mu = jnp.mean(xf, axis=-1, keepdims=True)
    var = jnp.mean(jnp.square(xf - mu), axis=-1, keepdims=True)
    y = (xf - mu) * lax.rsqrt(var + EPS)
    return (y * g.astype(jnp.float32) + b.astype(jnp.float32)).astype(x.dtype)


def dilated_branch(q, k, v, window, dilation):
    bsz, seq, nh, hd = q.shape
    sub_len = seq // dilation
    n_blk = -(-sub_len // ATTN_BLOCK)
    pad = n_blk * ATTN_BLOCK - sub_len
    sub_win = window // dilation

    def to_sub(t):
        t = t.reshape(bsz, sub_len, dilation, nh, hd).transpose(0, 2, 1, 3, 4)
        t = t.reshape(bsz * dilation, sub_len, nh, hd)
        t = jnp.pad(t, ((0, 0), (0, pad), (0, 0), (0, 0)))
        return t.reshape(bsz * dilation, n_blk, ATTN_BLOCK, nh, hd)

    def with_prev(t):
        prev = jnp.pad(t[:, :-1], ((0, 0), (1, 0), (0, 0), (0, 0), (0, 0)))
        return jnp.concatenate([prev, t], axis=2)

    qb, kb, vb = to_sub(q), to_sub(k), to_sub(v)
    kk, vv = with_prev(kb), with_prev(vb)
    s = jnp.einsum('bnihd,bnjhd->bnhij', qb, kk).astype(jnp.float32) * (hd ** -0.5)
    i = jnp.arange(ATTN_BLOCK)[:, None]
    j = jnp.arange(2 * ATTN_BLOCK)[None, :]
    dist = i + ATTN_BLOCK - j
    band = (dist >= 0) & (dist <= sub_win)
    key_exists = (jnp.arange(n_blk)[:, None, None] > 0) | (j[None] >= ATTN_BLOCK)
    mask = band[None] & key_exists
    s = jnp.where(mask[None, :, None], s, -jnp.inf)
    m = jnp.max(s, axis=-1, keepdims=True)
    p = jnp.exp(s - m)
    den = jnp.sum(p, axis=-1, keepdims=True)
    o = jnp.einsum('bnhij,bnjhd->bnihd', p, vv.astype(jnp.float32))
    o = o / den.transpose(0, 1, 3, 2, 4)
    lse = (m + jnp.log(den))[..., 0].transpose(0, 1, 3, 2)

    def from_sub(t):
        t = t.reshape((bsz * dilation, n_blk * ATTN_BLOCK) + t.shape[3:])[:, :sub_len]
        t = t.reshape((bsz, dilation, sub_len) + t.shape[2:])
        t = jnp.swapaxes(t, 1, 2)
        return t.reshape((bsz, seq) + t.shape[3:])

    return from_sub(o), from_sub(lse)


def dilated_attention(q, k, v):
    outs, lses = [], []
    for window, dilation in DILATED_PAIRS:
        o, lse = dilated_branch(q, k, v, window, dilation)
        outs.append(o)
        lses.append(lse)
    w = jax.nn.softmax(jnp.stack(lses, axis=0), axis=0)
    out = w[0][..., None] * outs[0]
    for n in range(1, len(outs)):
        out = out + w[n][..., None] * outs[n]
    return out


def mixer_ab(h, w_in, w_s, b_s, ln_g, ln_b, w_out):
    bsz, seq, _ = h.shape
    z = h @ w_in
    q, k, v, u, g = jnp.split(z, [WIDTH_A, 2 * WIDTH_A, 3 * WIDTH_A, 3 * WIDTH_A + WIDTH_B], axis=-1)
    hs = (bsz, seq, N_HEADS_A, HEAD_DIM_A)
    a = dilated_attention(q.reshape(hs), k.reshape(hs), v.reshape(hs))
    a = a.reshape(bsz, seq, WIDTH_A).astype(h.dtype)
    u = jax.nn.gelu(u)
    g = layer_norm(jax.nn.gelu(g), ln_g, ln_b)
    g = g.reshape(bsz, seq // CHUNK_B, CHUNK_B, N_GROUPS_B, GROUP_DIM_B)
    tril = jnp.tril(jnp.ones((CHUNK_B, CHUNK_B), dtype=bool))
    w_causal = jnp.where(tril[None], w_s, jnp.zeros_like(w_s))
    sp = jnp.einsum('hij,bcjhe->bcihe', w_causal, g) + b_s[:, :, None]
    bm = u * sp.reshape(bsz, seq, WIDTH_B)
    return jnp.concatenate([a, bm], axis=-1) @ w_out


def conformer_conv(h, w_pw1, b_pw1, w_dw, b_dw, ln_g, ln_b, w_pw2, b_pw2):
    y = jax.nn.glu(h @ w_pw1 + b_pw1, axis=-1)
    y = lax.conv_general_dilated(
        y, w_dw[:, None, :], window_strides=(1,), padding=((CONV_WIDTH - 1, 0),),
        dimension_numbers=('NWC', 'WIO', 'NWC'), feature_group_count=D_MODEL) + b_dw
    y = jax.nn.silu(layer_norm(y, ln_g, ln_b))
    return y @ w_pw2 + b_pw2


def swiglu(h, w_gate_up, w_down):
    gt, up = jnp.split(h @ w_gate_up, 2, axis=-1)
    return (jax.nn.silu(gt) * up) @ w_down


def setup_inputs(seed: int = 0) -> dict:
    key = jax.random.key(seed)
    ks = jax.random.split(key, 21)

    def nrm(k, shape, scale):
        return jax.random.normal(k, shape, jnp.float32) * scale

    D = D_MODEL
    return {
        "x": nrm(ks[0], (BATCH, SEQ, D), 1.0),
        "c": nrm(ks[1], (BATCH, D), 1.0),
        "w_mod": nrm(ks[2], (DEPTH, D, 6 * D), 0.5 * D ** -0.5),
        "b_mod": nrm(ks[3], (DEPTH, 6 * D), 0.02),
        "norm_g": 1.0 + nrm(ks[4], (DEPTH, 4, D), 0.05),
        "w_in_ab": nrm(ks[5], (N_EVEN, D, AB_IN), D ** -0.5),
        "w_s": nrm(ks[6], (N_EVEN, N_GROUPS_B, CHUNK_B, CHUNK_B), 0.5 * CHUNK_B ** -0.5),
        "b_s": 1.0 + nrm(ks[7], (N_EVEN, CHUNK_B, N_GROUPS_B), 0.1),
        "ln_v_g": 1.0 + nrm(ks[8], (N_EVEN, WIDTH_B), 0.05),
        "ln_v_b": nrm(ks[9], (N_EVEN, WIDTH_B), 0.02),
        "w_out_ab": nrm(ks[10], (N_EVEN, AB_OUT, D), AB_OUT ** -0.5),
        "w_pw1": nrm(ks[11], (N_ODD, D, 2 * D), D ** -0.5),
        "b_pw1": nrm(ks[12], (N_ODD, 2 * D), 0.02),
        "w_dw": nrm(ks[13], (N_ODD, CONV_WIDTH, D), CONV_WIDTH ** -0.5),
        "b_dw": nrm(ks[14], (N_ODD, D), 0.02),
        "ln_c_g": 1.0 + nrm(ks[15], (N_ODD, D), 0.05),
        "ln_c_b": nrm(ks[16], (N_ODD, D), 0.02),
        "w_pw2": nrm(ks[17], (N_ODD, D, D), D ** -0.5),
        "b_pw2": nrm(ks[18], (N_ODD, D), 0.02),
        "w_gate_up": nrm(ks[19], (DEPTH, D, 2 * D_FF), D ** -0.5),
        "w_down": nrm(ks[20], (DEPTH, D_FF, D), D_FF ** -0.5),
    }


def reference(x, c, w_mod, b_mod, norm_g, w_in_ab, w_s, b_s, ln_v_g, ln_v_b, w_out_ab,
              w_pw1, b_pw1, w_dw, b_dw, ln_c_g, ln_c_b, w_pw2, b_pw2, w_gate_up, w_down):
    c_act = jax.nn.silu(c)
    for layer in range(DEPTH):
        mod = (c_act @ w_mod[layer] + b_mod[layer])[:, None, :]
        sh1, sc1, gt1, sh2, sc2, gt2 = jnp.split(mod, 6, axis=-1)
        h = rms_norm(x, norm_g[layer, 0]) * (1.0 + sc1) + sh1
        if layer % 2 == 0:
            e = layer // 2
            y = mixer_ab(h, w_in_ab[e], w_s[e], b_s[e], ln_v_g[e], ln_v_b[e], w_out_ab[e])
        else:
            o = layer // 2
            y = conformer_conv(h, w_pw1[o], b_pw1[o], w_dw[o], b_dw[o], ln_c_g[o], ln_c_b[o],
                               w_pw2[o], b_pw2[o])
        x = x + gt1 * rms_norm(y, norm_g[layer, 1])
        h = rms_norm(x, norm_g[layer, 2]) * (1.0 + sc2) + sh2
        y = swiglu(h, w_gate_up[layer], w_down[layer])
        x = x + gt2 * rms_norm(y, norm_g[layer, 3])
    return x
```

```python
import functools

import jax
import jax.numpy as jnp
from jax import lax
from jax.experimental import pallas as pl
from jax.experimental.pallas import tpu as pltpu

F32 = jnp.float32
BF16 = jnp.bfloat16

D_MODEL = 2048
N_HEADS = 8
HEAD_DIM = 128
WIDTH_A = N_HEADS * HEAD_DIM
N_GROUPS = 8
GROUP_DIM = 128
WIDTH_B = N_GROUPS * GROUP_DIM
CHUNK = 128
ATTN_BLOCK = 128
DILATED_PAIRS = ((128, 1), (512, 4), (2048, 16))
AB_IN = 3 * WIDTH_A + 2 * WIDTH_B
CONV_WIDTH = 31
CONV_HALO = 32
D_FF = 5632
EPS = 1e-6

VMEM_LIMIT_BYTES = 56 * 1024 * 1024

TM_IN = 512
TM_OUT = 256
TM_CONF = 256
TM_FFN = 512
TF_FFN = 512
TN_MOD = 1024
TQ_MAX = 1024


def _params(*sem):
    return pltpu.CompilerParams(dimension_semantics=sem, vmem_limit_bytes=VMEM_LIMIT_BYTES)


def _resident(shape):
    return pl.BlockSpec(shape, lambda *_: (0,) * len(shape), pipeline_mode=pl.Buffered(1))


def _rms(x, g):
    return x * lax.rsqrt(jnp.mean(x * x, axis=-1, keepdims=True) + EPS) * g


def _layer_norm(x, g, b):
    mu = jnp.mean(x, axis=-1, keepdims=True)
    xc = x - mu
    var = jnp.mean(xc * xc, axis=-1, keepdims=True)
    return xc * lax.rsqrt(var + EPS) * g + b


def _modulated_norm(x, g, shift, scale):
    return _rms(x, g) * (1.0 + scale) + shift


def _mod_kernel(c_ref, w_ref, b_ref, o_ref):
    c_act = jax.nn.silu(c_ref[...]).astype(BF16)
    o_ref[...] = jnp.dot(c_act, w_ref[...].astype(BF16), preferred_element_type=F32) + b_ref[...]


def _modulation(c, w_mod, b_mod):
    bsz = c.shape[0]
    depth, _, n_out = w_mod.shape
    return pl.pallas_call(
        _mod_kernel,
        grid=(depth, n_out // TN_MOD),
        in_specs=[
            pl.BlockSpec((bsz, D_MODEL), lambda l, j: (0, 0)),
            pl.BlockSpec((None, D_MODEL, TN_MOD), lambda l, j: (l, 0, j)),
            pl.BlockSpec((None, 1, TN_MOD), lambda l, j: (l, 0, j)),
        ],
        out_specs=pl.BlockSpec((None, bsz, TN_MOD), lambda l, j: (l, 0, j)),
        out_shape=jax.ShapeDtypeStruct((depth, bsz, n_out), F32),
        compiler_params=_params("arbitrary", "arbitrary"),
        name="modulation",
    )(c, w_mod, b_mod.reshape(depth, 1, n_out))


def _in_proj_kernel(x_ref, mod_ref, ng_ref, w_ref, lng_ref, lnb_ref, qkv_ref, ug_ref, h_scr):
    j = pl.program_id(1)

    @pl.when(j == 0)
    def _():
        h = _modulated_norm(x_ref[...], ng_ref[0:1, :], mod_ref[0:1, :], mod_ref[1:2, :])
        h_scr[...] = h.astype(BF16)

    z = jnp.dot(h_scr[...], w_ref[...], preferred_element_type=F32)

    @pl.when(j < 3)
    def _():
        qkv_ref[...] = z.astype(BF16)

    @pl.when(j == 3)
    def _():
        ug_ref[...] = jax.nn.gelu(z)

    @pl.when(j == 4)
    def _():
        ug_ref[...] = _layer_norm(jax.nn.gelu(z), lng_ref[...], lnb_ref[...])


def _in_proj(x2, mod_l, ng_l, w_in, ln_g, ln_b, seq):
    rows = x2.shape[0]
    tpb = seq // TM_IN
    return pl.pallas_call(
        _in_proj_kernel,
        grid=(rows // TM_IN, AB_IN // WIDTH_A),
        in_specs=[
            pl.BlockSpec((TM_IN, D_MODEL), lambda i, j: (i, 0)),
            pl.BlockSpec((None, 6, D_MODEL), lambda i, j: (i // tpb, 0, 0)),
            pl.BlockSpec((4, D_MODEL), lambda i, j: (0, 0)),
            pl.BlockSpec((D_MODEL, WIDTH_A), lambda i, j: (0, j)),
            pl.BlockSpec((1, WIDTH_B), lambda i, j: (0, 0)),
            pl.BlockSpec((1, WIDTH_B), lambda i, j: (0, 0)),
        ],
        out_specs=[
            pl.BlockSpec((TM_IN, WIDTH_A), lambda i, j: (i, jnp.minimum(j, 2))),
            pl.BlockSpec((TM_IN, WIDTH_B), lambda i, j: (i, jnp.maximum(j - 3, 0))),
        ],
        out_shape=[
            jax.ShapeDtypeStruct((rows, 3 * WIDTH_A), BF16),
            jax.ShapeDtypeStruct((rows, 2 * WIDTH_B), F32),
        ],
        scratch_shapes=[pltpu.VMEM((TM_IN, D_MODEL), BF16)],
        compiler_params=_params("arbitrary", "arbitrary"),
        name="in_proj",
    )(x2, mod_l, ng_l, w_in, ln_g.reshape(1, WIDTH_B), ln_b.reshape(1, WIDTH_B))


def _attn_kernel(q_ref, k_ref, v_ref, kp_ref, vp_ref, o_ref, lse_ref, k_scr, v_scr, *, tq):
    t = pl.program_id(2)
    blk = ATTN_BLOCK
    n_blk = tq // blk
    k_scr[0:blk, :] = kp_ref[...]
    v_scr[0:blk, :] = vp_ref[...]
    k_scr[blk:, :] = k_ref[...]
    v_scr[blk:, :] = v_ref[...]

    qi = lax.broadcasted_iota(jnp.int32, (blk, 2 * blk), 0)
    kj = lax.broadcasted_iota(jnp.int32, (blk, 2 * blk), 1)
    band = (kj >= qi) & (kj <= qi + blk)
    lane = lax.broadcasted_iota(jnp.int32, (blk, HEAD_DIM), 1)
    scale = HEAD_DIM ** -0.5

    def body(n, carry):
        r0 = pl.multiple_of(n * blk, blk)
        first_key = jnp.where(t * n_blk + n == 0, blk, 0)
        valid = band & (kj >= first_key)
        lse_tile = jnp.zeros((blk, HEAD_DIM), F32)
        for h in range(N_HEADS):
            hs = slice(h * HEAD_DIM, (h + 1) * HEAD_DIM)
            qh = q_ref[pl.ds(r0, blk), hs]
            kk = k_scr[pl.ds(r0, 2 * blk), hs]
            vv = v_scr[pl.ds(r0, 2 * blk), hs]
            s = lax.dot_general(qh, kk, (((1,), (1,)), ((), ())), preferred_element_type=F32) * scale
            s = jnp.where(valid, s, -jnp.inf)
            m = jnp.max(s, axis=-1, keepdims=True)
            p = jnp.exp(s - m)
            den = jnp.sum(p, axis=-1, keepdims=True)
            o = jnp.dot(p.astype(BF16), vv, preferred_element_type=F32) / den
            o_ref[pl.ds(r0, blk), hs] = o
            lse_tile = jnp.where(lane == h, m + jnp.log(den), lse_tile)
        lse_ref[pl.ds(r0, blk), :] = lse_tile
        return carry

    lax.fori_loop(0, n_blk, body, 0)


def _attn_branch(qkv, bsz, seq, dilation):
    sub = seq // dilation
    tq = min(sub, TQ_MAX)
    n_t = sub // tq
    per_tile = tq // ATTN_BLOCK
    qkv_v = qkv.reshape(bsz, sub, dilation * 3 * WIDTH_A)

    def prev_idx(t):
        return jnp.maximum(t * per_tile - 1, 0)

    o, lse = pl.pallas_call(
        functools.partial(_attn_kernel, tq=tq),
        grid=(bsz, dilation, n_t),
        in_specs=[
            pl.BlockSpec((None, tq, WIDTH_A), lambda b, r, t: (b, t, 3 * r)),
            pl.BlockSpec((None, tq, WIDTH_A), lambda b, r, t: (b, t, 3 * r + 1)),
            pl.BlockSpec((None, tq, WIDTH_A), lambda b, r, t: (b, t, 3 * r + 2)),
            pl.BlockSpec((None, ATTN_BLOCK, WIDTH_A), lambda b, r, t: (b, prev_idx(t), 3 * r + 1)),
            pl.BlockSpec((None, ATTN_BLOCK, WIDTH_A), lambda b, r, t: (b, prev_idx(t), 3 * r + 2)),
        ],
        out_specs=[
            pl.BlockSpec((None, tq, WIDTH_A), lambda b, r, t: (b, t, r)),
            pl.BlockSpec((None, tq, HEAD_DIM), lambda b, r, t: (b, t, r)),
        ],
        out_shape=[
            jax.ShapeDtypeStruct((bsz, sub, dilation * WIDTH_A), F32),
            jax.ShapeDtypeStruct((bsz, sub, dilation * HEAD_DIM), F32),
        ],
        scratch_shapes=[
            pltpu.VMEM((tq + ATTN_BLOCK, WIDTH_A), BF16),
            pltpu.VMEM((tq + ATTN_BLOCK, WIDTH_A), BF16),
        ],
        compiler_params=_params("arbitrary", "arbitrary", "arbitrary"),
        name=f"attn_d{dilation}",
    )(qkv_v, qkv_v, qkv_v, qkv_v, qkv_v)
    return o.reshape(bsz * seq, WIDTH_A), lse.reshape(bsz * seq, HEAD_DIM)


def _out_proj_kernel(x_ref, mod_ref, ng_ref, o1_ref, o2_ref, o3_ref, l1_ref, l2_ref, l3_ref,
                     u_ref, g_ref, ws_ref, bs_ref, w_ref, out_ref, cat_scr):
    tm = x_ref.shape[0]
    n_chunk = tm // CHUNK
    l1, l2, l3 = l1_ref[...], l2_ref[...], l3_ref[...]
    lm = jnp.maximum(jnp.maximum(l1, l2), l3)
    e1, e2, e3 = jnp.exp(l1 - lm), jnp.exp(l2 - lm), jnp.exp(l3 - lm)
    esum = e1 + e2 + e3
    w1, w2, w3 = e1 / esum, e2 / esum, e3 / esum
    for h in range(N_HEADS):
        hs = slice(h * HEAD_DIM, (h + 1) * HEAD_DIM)
        a = w1[:, h:h + 1] * o1_ref[:, hs]
        a = a + w2[:, h:h + 1] * o2_ref[:, hs]
        a = a + w3[:, h:h + 1] * o3_ref[:, hs]
        cat_scr[:, hs] = a.astype(BF16)

    ci = lax.broadcasted_iota(jnp.int32, (CHUNK, CHUNK), 0)
    cj = lax.broadcasted_iota(jnp.int32, (CHUNK, CHUNK), 1)
    tril = cj <= ci
    for h in range(N_GROUPS):
        hs = slice(h * GROUP_DIM, (h + 1) * GROUP_DIM)
        w_c = jnp.where(tril, ws_ref[h], 0.0).astype(BF16)
        g_side = jnp.concatenate(
            [g_ref[c * CHUNK:(c + 1) * CHUNK, hs] for c in range(n_chunk)], axis=1).astype(BF16)
        sp = jnp.dot(w_c, g_side, preferred_element_type=F32)
        for c in range(n_chunk):
            rs = slice(c * CHUNK, (c + 1) * CHUNK)
            sp_c = sp[:, c * GROUP_DIM:(c + 1) * GROUP_DIM] + bs_ref[:, hs]
            cat_scr[rs, WIDTH_A + h * GROUP_DIM:WIDTH_A + (h + 1) * GROUP_DIM] = (
                u_ref[rs, hs] * sp_c).astype(BF16)

    y = jnp.dot(cat_scr[...], w_ref[...], preferred_element_type=F32)
    out_ref[...] = x_ref[...] + mod_ref[2:3, :] * _rms(y, ng_ref[1:2, :])


def _out_proj(x2, mod_l, ng_l, branches, ug, w_s, b_s_wide, w_out, seq):
    rows = x2.shape[0]
    tm = TM_OUT
    tpb = seq // tm
    (o1, l1), (o2, l2), (o3, l3) = branches
    row_a = pl.BlockSpec((tm, WIDTH_A), lambda i: (i, 0))
    row_l = pl.BlockSpec((tm, HEAD_DIM), lambda i: (i, 0))
    return pl.pallas_call(
        _out_proj_kernel,
        grid=(rows // tm,),
        in_specs=[
            pl.BlockSpec((tm, D_MODEL), lambda i: (i, 0)),
            pl.BlockSpec((None, 6, D_MODEL), lambda i: (i // tpb, 0, 0)),
            pl.BlockSpec((4, D_MODEL), lambda i: (0, 0)),
            row_a, row_a, row_a, row_l, row_l, row_l,
            pl.BlockSpec((tm, WIDTH_B), lambda i: (i, 0)),
            pl.BlockSpec((tm, WIDTH_B), lambda i: (i, 1)),
            pl.BlockSpec((N_GROUPS, CHUNK, CHUNK), lambda i: (0, 0, 0)),
            pl.BlockSpec((CHUNK, WIDTH_B), lambda i: (0, 0)),
            _resident((WIDTH_A + WIDTH_B, D_MODEL)),
        ],
        out_specs=pl.BlockSpec((tm, D_MODEL), lambda i: (i, 0)),
        out_shape=jax.ShapeDtypeStruct((rows, D_MODEL), F32),
        scratch_shapes=[pltpu.VMEM((tm, WIDTH_A + WIDTH_B), BF16)],
        compiler_params=_params("arbitrary"),
        name="out_proj",
    )(x2, mod_l, ng_l, o1, o2, o3, l1, l2, l3, ug, ug, w_s, b_s_wide, w_out)


def _conformer_kernel(x_ref, mod_ref, ng_ref, w1_ref, b1_ref, wdw_ref, bdw_ref, lng_ref, lnb_ref,
                      w2_ref, b2_ref, out_ref, y_scr, *, tiles_per_seq):
    tm = x_ref.shape[0]
    i = pl.program_id(0)

    @pl.when(i % tiles_per_seq == 0)
    def _():
        y_scr[0:CONV_HALO, :] = jnp.zeros((CONV_HALO, D_MODEL), F32)

    h = _modulated_norm(x_ref[...], ng_ref[0:1, :], mod_ref[0:1, :], mod_ref[1:2, :]).astype(BF16)
    z = jnp.dot(h, w1_ref[...], preferred_element_type=F32) + b1_ref[...]
    y_scr[CONV_HALO:, :] = z[:, :D_MODEL] * jax.nn.sigmoid(z[:, D_MODEL:])

    base = CONV_HALO - (CONV_WIDTH - 1)
    acc = jnp.zeros((tm, D_MODEL), F32)
    for k in range(CONV_WIDTH):
        acc = acc + wdw_ref[k:k + 1, :] * y_scr[base + k:base + k + tm, :]
    acc = acc + bdw_ref[...]
    y_scr[0:CONV_HALO, :] = y_scr[tm:tm + CONV_HALO, :]

    a = jax.nn.silu(_layer_norm(acc, lng_ref[...], lnb_ref[...])).astype(BF16)
    y = jnp.dot(a, w2_ref[...], preferred_element_type=F32) + b2_ref[...]
    out_ref[...] = x_ref[...] + mod_ref[2:3, :] * _rms(y, ng_ref[1:2, :])


def _conformer(x2, mod_l, ng_l, w1, b1, w_dw, b_dw, ln_g, ln_b, w2, b2, seq):
    rows = x2.shape[0]
    tm = TM_CONF
    tpb = seq // tm
    vec = lambda n: pl.BlockSpec((1, n), lambda i: (0, 0))
    return pl.pallas_call(
        functools.partial(_conformer_kernel, tiles_per_seq=tpb),
        grid=(rows // tm,),
        in_specs=[
            pl.BlockSpec((tm, D_MODEL), lambda i: (i, 0)),
            pl.BlockSpec((None, 6, D_MODEL), lambda i: (i // tpb, 0, 0)),
            pl.BlockSpec((4, D_MODEL), lambda i: (0, 0)),
            _resident((D_MODEL, 2 * D_MODEL)),
            vec(2 * D_MODEL),
            pl.BlockSpec((CONV_WIDTH, D_MODEL), lambda i: (0, 0)),
            vec(D_MODEL), vec(D_MODEL), vec(D_MODEL),
            _resident((D_MODEL, D_MODEL)),
            vec(D_MODEL),
        ],
        out_specs=pl.BlockSpec((tm, D_MODEL), lambda i: (i, 0)),
        out_shape=jax.ShapeDtypeStruct((rows, D_MODEL), F32),
        scratch_shapes=[pltpu.VMEM((CONV_HALO + tm, D_MODEL), F32)],
        compiler_params=_params("arbitrary"),
        name="conformer",
    )(x2, mod_l, ng_l, w1, b1.reshape(1, -1), w_dw, b_dw.reshape(1, -1), ln_g.reshape(1, -1),
      ln_b.reshape(1, -1), w2, b2.reshape(1, -1))


def _ffn_kernel(x_ref, mod_ref, ng_ref, wg_ref, wu_ref, wd_ref, out_ref, h_scr, acc_scr):
    j = pl.program_id(1)

    @pl.when(j == 0)
    def _():
        h = _modulated_norm(x_ref[...], ng_ref[2:3, :], mod_ref[3:4, :], mod_ref[4:5, :])
        h_scr[...] = h.astype(BF16)
        acc_scr[...] = jnp.zeros_like(acc_scr)

    h = h_scr[...]
    gate = jnp.dot(h, wg_ref[...], preferred_element_type=F32)
    up = jnp.dot(h, wu_ref[...], preferred_element_type=F32)
    a = (jax.nn.silu(gate) * up).astype(BF16)
    acc_scr[...] += jnp.dot(a, wd_ref[...], preferred_element_type=F32)

    @pl.when(j == pl.num_programs(1) - 1)
    def _():
        out_ref[...] = x_ref[...] + mod_ref[5:6, :] * _rms(acc_scr[...], ng_ref[3:4, :])


def _ffn(x2, mod_l, ng_l, w_gu, w_down, seq):
    rows = x2.shape[0]
    tm, tf = TM_FFN, TF_FFN
    tpb = seq // tm
    n_f = D_FF // tf
    return pl.pallas_call(
        _ffn_kernel,
        grid=(rows // tm, n_f),
        in_specs=[
            pl.BlockSpec((tm, D_MODEL), lambda i, j: (i, 0)),
            pl.BlockSpec((None, 6, D_MODEL), lambda i, j: (i // tpb, 0, 0)),
            pl.BlockSpec((4, D_MODEL), lambda i, j: (0, 0)),
            pl.BlockSpec((D_MODEL, tf), lambda i, j: (0, j)),
            pl.BlockSpec((D_MODEL, tf), lambda i, j: (0, j + n_f)),
            pl.BlockSpec((tf, D_MODEL), lambda i, j: (j, 0)),
        ],
        out_specs=pl.BlockSpec((tm, D_MODEL), lambda i, j: (i, 0)),
        out_shape=jax.ShapeDtypeStruct((rows, D_MODEL), F32),
        scratch_shapes=[pltpu.VMEM((tm, D_MODEL), BF16), pltpu.VMEM((tm, D_MODEL), F32)],
        compiler_params=_params("arbitrary", "arbitrary"),
        name="ffn",
    )(x2, mod_l, ng_l, w_gu, w_gu, w_down)


def kernel(x, c, w_mod, b_mod, norm_g, w_in_ab, w_s, b_s, ln_v_g, ln_v_b, w_out_ab, w_pw1, b_pw1, w_dw,
           b_dw, ln_c_g, ln_c_b, w_pw2, b_pw2, w_gate_up, w_down):
    bsz, seq, _ = x.shape
    depth = w_mod.shape[0]
    mod = _modulation(c, w_mod, b_mod).reshape(depth, bsz, 6, D_MODEL)
    x2 = x.reshape(bsz * seq, D_MODEL)
    for layer in range(depth):
        mod_l, ng_l = mod[layer], norm_g[layer]
        if layer % 2 == 0:
            e = layer // 2
            qkv, ug = _in_proj(x2, mod_l, ng_l, w_in_ab[e].astype(BF16), ln_v_g[e], ln_v_b[e], seq)
            branches = [_attn_branch(qkv, bsz, seq, d) for _, d in DILATED_PAIRS]
            b_s_wide = jnp.repeat(b_s[e], GROUP_DIM, axis=1)
            x2 = _out_proj(x2, mod_l, ng_l, branches, ug, w_s[e], b_s_wide, w_out_ab[e].astype(BF16), seq)
        else:
            o = layer // 2
            x2 = _conformer(x2, mod_l, ng_l, w_pw1[o].astype(BF16), b_pw1[o], w_dw[o], b_dw[o],
                            ln_c_g[o], ln_c_b[o], w_pw2[o].astype(BF16), b_pw2[o], seq)
        x2 = _ffn(x2, mod_l, ng_l, w_gate_up[layer].astype(BF16), w_down[layer].astype(BF16), seq)
    return x2.reshape(bsz, seq, D_MODEL)
```

```python
import functools

import jax
import jax.numpy as jnp
from jax import lax
from jax.experimental import pallas as pl
from jax.experimental.pallas import tpu as pltpu

F32 = jnp.float32
BF16 = jnp.bfloat16

D_MODEL = 2048
N_HEADS = 8
HEAD_DIM = 128
WIDTH_A = N_HEADS * HEAD_DIM
N_GROUPS = 8
GROUP_DIM = 128
WIDTH_B = N_GROUPS * GROUP_DIM
CHUNK = 128
ATTN_BLOCK = 128
DILATIONS = (1, 4, 16)
AB_IN = 3 * WIDTH_A + 2 * WIDTH_B
CONV_WIDTH = 31
CONV_HALO = 32
D_FF = 5632
EPS = 1e-6
LANES = 128

VMEM_LIMIT_BYTES = 56 * 1024 * 1024

TM_IN = 256
TM_OUT = 256
TM_CONF = 256
CONF_TN = 256
TM_FFN = 512
TF_FFN = 512
TN_MOD = 1024
ROW_CHUNK = 16
ROW_UNROLL = 4
ATTN_UNROLL = 8


def _params(*sem):
    return pltpu.CompilerParams(dimension_semantics=sem, vmem_limit_bytes=VMEM_LIMIT_BYTES)


def _resident(shape):
    return pl.BlockSpec(shape, lambda *_: (0,) * len(shape), pipeline_mode=pl.Buffered(1))


def _mod_spec(layer, tpb):
    return pl.BlockSpec((None, None, 6, D_MODEL), lambda i, *_: (layer, i // tpb, 0, 0))


def _ng_spec(layer):
    return pl.BlockSpec((None, 4, D_MODEL), lambda *_: (layer, 0, 0))


def _layer_norm(x, g, b):
    mu = jnp.mean(x, axis=-1, keepdims=True)
    xc = x - mu
    var = jnp.mean(xc * xc, axis=-1, keepdims=True)
    return xc * lax.rsqrt(var + EPS) * g + b


def _modnorm_to(x_ref, ng_ref, g_row, mod_ref, shift_row, scale_row, h_ref):
    def body(c, carry):
        rows = pl.ds(pl.multiple_of(c * ROW_CHUNK, ROW_CHUNK), ROW_CHUNK)
        x = x_ref[rows, :]
        rinv = lax.rsqrt(jnp.mean(x * x, axis=-1, keepdims=True) + EPS)
        gs = ng_ref[g_row:g_row + 1, :] * (1.0 + mod_ref[scale_row:scale_row + 1, :])
        h_ref[rows, :] = (x * rinv * gs + mod_ref[shift_row:shift_row + 1, :]).astype(BF16)
        return carry

    lax.fori_loop(0, x_ref.shape[0] // ROW_CHUNK, body, 0, unroll=ROW_UNROLL)


def _residual_to(x_ref, y_ref, ng_ref, g_row, mod_ref, gate_row, out_ref):
    def body(c, carry):
        rows = pl.ds(pl.multiple_of(c * ROW_CHUNK, ROW_CHUNK), ROW_CHUNK)
        y = y_ref[rows, :]
        rinv = lax.rsqrt(jnp.mean(y * y, axis=-1, keepdims=True) + EPS)
        gg = ng_ref[g_row:g_row + 1, :] * mod_ref[gate_row:gate_row + 1, :]
        out_ref[rows, :] = x_ref[rows, :] + y * rinv * gg
        return carry

    lax.fori_loop(0, x_ref.shape[0] // ROW_CHUNK, body, 0, unroll=ROW_UNROLL)


def _mod_kernel(c_ref, w_ref, b_ref, o_ref):
    c_act = jax.nn.silu(c_ref[...]).astype(BF16)
    o_ref[...] = jnp.dot(c_act, w_ref[...].astype(BF16), preferred_element_type=F32) + b_ref[...]


def _modulation(c, w_mod, b_mod):
    bsz = c.shape[0]
    depth, _, n_out = w_mod.shape
    return pl.pallas_call(
        _mod_kernel,
        grid=(depth, n_out // TN_MOD),
        in_specs=[
            pl.BlockSpec((bsz, D_MODEL), lambda l, j: (0, 0)),
            pl.BlockSpec((None, D_MODEL, TN_MOD), lambda l, j: (l, 0, j)),
            pl.BlockSpec((None, 1, TN_MOD), lambda l, j: (l, 0, j)),
        ],
        out_specs=pl.BlockSpec((None, bsz, TN_MOD), lambda l, j: (l, 0, j)),
        out_shape=jax.ShapeDtypeStruct((depth, bsz, n_out), F32),
        compiler_params=_params("arbitrary", "arbitrary"),
        name="modulation",
    )(c, w_mod, b_mod.reshape(depth, 1, n_out))


def _in_proj_kernel(x_ref, mod_ref, ng_ref, w_ref, lng_ref, lnb_ref, z_ref, h_scr):
    _modnorm_to(x_ref, ng_ref, 0, mod_ref, 0, 1, h_scr)
    h = h_scr[...]
    for j in range(AB_IN // WIDTH_A):
        cols = slice(j * WIDTH_A, (j + 1) * WIDTH_A)
        z = jnp.dot(h, w_ref[:, cols], preferred_element_type=F32)
        if j >= 3:
            z = jax.nn.gelu(z)
        if j == 4:
            z = _layer_norm(z, lng_ref[...], lnb_ref[...])
        z_ref[:, cols] = z


def _in_proj(x2, mod, norm_g, layer, w_in, ln_g, ln_b, seq):
    rows = x2.shape[0]
    tm = TM_IN
    return pl.pallas_call(
        _in_proj_kernel,
        grid=(rows // tm,),
        in_specs=[
            pl.BlockSpec((tm, D_MODEL), lambda i: (i, 0)),
            _mod_spec(layer, seq // tm),
            _ng_spec(layer),
            _resident((D_MODEL, AB_IN)),
            pl.BlockSpec((1, WIDTH_B), lambda i: (0, 0)),
            pl.BlockSpec((1, WIDTH_B), lambda i: (0, 0)),
        ],
        out_specs=pl.BlockSpec((tm, AB_IN), lambda i: (i, 0)),
        out_shape=jax.ShapeDtypeStruct((rows, AB_IN), F32),
        scratch_shapes=[pltpu.VMEM((tm, D_MODEL), BF16)],
        compiler_params=_params("arbitrary"),
        name="in_proj",
    )(x2, mod, norm_g, w_in, ln_g.reshape(1, WIDTH_B), ln_b.reshape(1, WIDTH_B))


def _rows(start, size, stride):
    return pl.ds(start, size) if stride == 1 else pl.ds(start, size, stride=stride)


def _attn_kernel(q_ref, k_ref, v_ref, a_ref, k_pad, v_pad, o1, o2, o3, l1, l2, l3):
    seq = q_ref.shape[0]
    blk = ATTN_BLOCK
    pad = blk * max(DILATIONS)
    k_pad[0:pad, :] = jnp.zeros((pad, HEAD_DIM), F32)
    v_pad[0:pad, :] = jnp.zeros((pad, HEAD_DIM), F32)
    k_pad[pad:, :] = k_ref[...]
    v_pad[pad:, :] = v_ref[...]

    qi = lax.broadcasted_iota(jnp.int32, (blk, 2 * blk), 0)
    kj = lax.broadcasted_iota(jnp.int32, (blk, 2 * blk), 1)
    band = (kj >= qi) & (kj <= qi + blk)
    scale = HEAD_DIM ** -0.5

    for d, o_scr, l_scr in zip(DILATIONS, (o1, o2, o3), (l1, l2, l3)):
        per_class = seq // (blk * d)
        shift = per_class.bit_length() - 1

        def body(idx, carry, d=d, o_scr=o_scr, l_scr=l_scr, per_class=per_class, shift=shift):
            r = lax.shift_right_logical(idx, shift)
            m = lax.bitwise_and(idx, per_class - 1)
            start = r + m * (blk * d)
            q = q_ref[_rows(start, blk, d), :].astype(BF16)
            kk = k_pad[_rows(pad + start - blk * d, 2 * blk, d), :].astype(BF16)
            vv = v_pad[_rows(pad + start - blk * d, 2 * blk, d), :].astype(BF16)
            s = lax.dot_general(q, kk, (((1,), (1,)), ((), ())), preferred_element_type=F32) * scale
            first_key = jnp.where(m == 0, blk, 0)
            s = jnp.where(band & (kj >= first_key), s, -jnp.inf)
            mx = jnp.max(s, axis=-1, keepdims=True)
            p = jnp.exp(s - mx)
            den = jnp.sum(p, axis=-1, keepdims=True)
            o = jnp.dot(p.astype(BF16), vv, preferred_element_type=F32) / den
            o_scr[_rows(start, blk, d), :] = o
            l_scr[_rows(start, blk, d), :] = jnp.broadcast_to(mx + jnp.log(den), (blk, HEAD_DIM))
            return carry

        lax.fori_loop(0, seq // blk, body, 0, unroll=ATTN_UNROLL)

    def mix(c, carry):
        rows = pl.ds(pl.multiple_of(c * blk, blk), blk)
        la, lb, lc = l1[rows, :], l2[rows, :], l3[rows, :]
        lm = jnp.maximum(jnp.maximum(la, lb), lc)
        ea, eb, ec = jnp.exp(la - lm), jnp.exp(lb - lm), jnp.exp(lc - lm)
        esum = ea + eb + ec
        out = (ea / esum) * o1[rows, :] + (eb / esum) * o2[rows, :] + (ec / esum) * o3[rows, :]
        a_ref[rows, :] = out.astype(BF16)
        return carry

    lax.fori_loop(0, seq // blk, mix, 0)


def _attention(z, bsz, seq):
    z3 = z.reshape(bsz, seq, AB_IN)
    pad = ATTN_BLOCK * max(DILATIONS)
    head = lambda part: pl.BlockSpec((None, seq, HEAD_DIM), lambda b, h: (b, 0, part * N_HEADS + h))
    a = pl.pallas_call(
        _attn_kernel,
        grid=(bsz, N_HEADS),
        in_specs=[head(0), head(1), head(2)],
        out_specs=pl.BlockSpec((None, seq, HEAD_DIM), lambda b, h: (b, 0, h)),
        out_shape=jax.ShapeDtypeStruct((bsz, seq, WIDTH_A), BF16),
        scratch_shapes=[pltpu.VMEM((pad + seq, HEAD_DIM), F32)] * 2 + [pltpu.VMEM((seq, HEAD_DIM), F32)] * 6,
        compiler_params=_params("arbitrary", "arbitrary"),
        name="attention",
    )(z3, z3, z3)
    return a.reshape(bsz * seq, WIDTH_A)


def _out_proj_kernel(x_ref, mod_ref, ng_ref, a_ref, u_ref, g_ref, ws_ref, bs_ref, w_ref, out_ref,
                     cat_scr, y_scr):
    tm = x_ref.shape[0]
    n_chunk = tm // CHUNK
    cat_scr[:, 0:WIDTH_A] = a_ref[...]

    ci = lax.broadcasted_iota(jnp.int32, (CHUNK, CHUNK), 0)
    cj = lax.broadcasted_iota(jnp.int32, (CHUNK, CHUNK), 1)
    tril = cj <= ci
    for h in range(N_GROUPS):
        hs = slice(h * GROUP_DIM, (h + 1) * GROUP_DIM)
        w_c = jnp.where(tril, ws_ref[h], 0.0).astype(BF16)
        g_side = jnp.concatenate(
            [g_ref[c * CHUNK:(c + 1) * CHUNK, hs] for c in range(n_chunk)], axis=1).astype(BF16)
        sp = jnp.dot(w_c, g_side, preferred_element_type=F32)
        for c in range(n_chunk):
            rs = slice(c * CHUNK, (c + 1) * CHUNK)
            sp_c = sp[:, c * GROUP_DIM:(c + 1) * GROUP_DIM] + bs_ref[:, hs]
            cat_scr[rs, WIDTH_A + h * GROUP_DIM:WIDTH_A + (h + 1) * GROUP_DIM] = (
                u_ref[rs, hs] * sp_c).astype(BF16)

    y_scr[...] = jnp.dot(cat_scr[...], w_ref[...], preferred_element_type=F32)
    _residual_to(x_ref, y_scr, ng_ref, 1, mod_ref, 2, out_ref)


def _out_proj(x2, mod, norm_g, layer, a, z, w_s, b_s_wide, w_out, seq):
    rows = x2.shape[0]
    tm = TM_OUT
    return pl.pallas_call(
        _out_proj_kernel,
        grid=(rows // tm,),
        in_specs=[
            pl.BlockSpec((tm, D_MODEL), lambda i: (i, 0)),
            _mod_spec(layer, seq // tm),
            _ng_spec(layer),
            pl.BlockSpec((tm, WIDTH_A), lambda i: (i, 0)),
            pl.BlockSpec((tm, WIDTH_B), lambda i: (i, 3)),
            pl.BlockSpec((tm, WIDTH_B), lambda i: (i, 4)),
            pl.BlockSpec((N_GROUPS, CHUNK, CHUNK), lambda i: (0, 0, 0)),
            pl.BlockSpec((CHUNK, WIDTH_B), lambda i: (0, 0)),
            _resident((WIDTH_A + WIDTH_B, D_MODEL)),
        ],
        out_specs=pl.BlockSpec((tm, D_MODEL), lambda i: (i, 0)),
        out_shape=jax.ShapeDtypeStruct((rows, D_MODEL), F32),
        scratch_shapes=[pltpu.VMEM((tm, WIDTH_A + WIDTH_B), BF16), pltpu.VMEM((tm, D_MODEL), F32)],
        compiler_params=_params("arbitrary"),
        name="out_proj",
    )(x2, mod, norm_g, a, z, z, w_s, b_s_wide, w_out)


def _conformer_kernel(x_ref, mod_ref, ng_ref, w1_ref, b1_ref, wdw_ref, bdw_ref, lng_ref, lnb_ref,
                      w2_ref, b2_ref, out_ref, h_scr, glu_scr, conv_scr, y_scr, *, tiles_per_seq):
    tm = x_ref.shape[0]
    n_slab = D_MODEL // LANES

    @pl.when(pl.program_id(0) % tiles_per_seq == 0)
    def _():
        glu_scr[:, 0:CONV_HALO, :] = jnp.zeros((n_slab, CONV_HALO, LANES), F32)

    _modnorm_to(x_ref, ng_ref, 0, mod_ref, 0, 1, h_scr)
    base = CONV_HALO - (CONV_WIDTH - 1)
    slabs_per = CONF_TN // LANES
    n_col = D_MODEL // CONF_TN

    def glu_chunk(jj):
        c0 = pl.multiple_of(jj * CONF_TN, CONF_TN)
        h = h_scr[...]
        za = jnp.dot(h, w1_ref[:, pl.ds(c0, CONF_TN)], preferred_element_type=F32) + b1_ref[:, pl.ds(c0, CONF_TN)]
        zb = (jnp.dot(h, w1_ref[:, pl.ds(D_MODEL + c0, CONF_TN)], preferred_element_type=F32)
              + b1_ref[:, pl.ds(D_MODEL + c0, CONF_TN)])
        glu = za * jax.nn.sigmoid(zb)
        for s in range(slabs_per):
            glu_scr[jj * slabs_per + s, CONV_HALO:, :] = glu[:, s * LANES:(s + 1) * LANES]

    def conv_chunk(jj):
        c0 = pl.multiple_of(jj * CONF_TN, CONF_TN)
        for s in range(slabs_per):
            j = jj * slabs_per + s
            cols = pl.ds(c0 + s * LANES, LANES)
            acc = jnp.broadcast_to(bdw_ref[:, cols], (tm, LANES))
            for k in range(CONV_WIDTH):
                acc = acc + wdw_ref[k:k + 1, cols] * glu_scr[j, pl.ds(base + k, tm), :]
            conv_scr[:, cols] = acc
            glu_scr[j, 0:CONV_HALO, :] = glu_scr[j, tm:tm + CONV_HALO, :]

    glu_chunk(0)
    for jj in range(1, n_col):
        conv_chunk(jj - 1)
        glu_chunk(jj)
    conv_chunk(n_col - 1)

    def ln_body(c, carry):
        rows = pl.ds(pl.multiple_of(c * ROW_CHUNK, ROW_CHUNK), ROW_CHUNK)
        a = jax.nn.silu(_layer_norm(conv_scr[rows, :], lng_ref[...], lnb_ref[...]))
        h_scr[rows, :] = a.astype(BF16)
        return carry

    lax.fori_loop(0, tm // ROW_CHUNK, ln_body, 0, unroll=ROW_UNROLL)
    y_scr[...] = jnp.dot(h_scr[...], w2_ref[...], preferred_element_type=F32) + b2_ref[...]
    _residual_to(x_ref, y_scr, ng_ref, 1, mod_ref, 2, out_ref)


def _conformer(x2, mod, norm_g, layer, w1, b1, w_dw, b_dw, ln_g, ln_b, w2, b2, seq):
    rows = x2.shape[0]
    tm = TM_CONF
    vec = lambda n: pl.BlockSpec((1, n), lambda i: (0, 0))
    return pl.pallas_call(
        functools.partial(_conformer_kernel, tiles_per_seq=seq // tm),
        grid=(rows // tm,),
        in_specs=[
            pl.BlockSpec((tm, D_MODEL), lambda i: (i, 0)),
            _mod_spec(layer, seq // tm),
            _ng_spec(layer),
            _resident((D_MODEL, 2 * D_MODEL)),
            vec(2 * D_MODEL),
            pl.BlockSpec((CONV_WIDTH, D_MODEL), lambda i: (0, 0)),
            vec(D_MODEL), vec(D_MODEL), vec(D_MODEL),
            _resident((D_MODEL, D_MODEL)),
            vec(D_MODEL),
        ],
        out_specs=pl.BlockSpec((tm, D_MODEL), lambda i: (i, 0)),
        out_shape=jax.ShapeDtypeStruct((rows, D_MODEL), F32),
        scratch_shapes=[
            pltpu.VMEM((tm, D_MODEL), BF16),
            pltpu.VMEM((D_MODEL // LANES, CONV_HALO + tm, LANES), F32),
            pltpu.VMEM((tm, D_MODEL), F32),
            pltpu.VMEM((tm, D_MODEL), F32),
        ],
        compiler_params=_params("arbitrary"),
        name="conformer",
    )(x2, mod, norm_g, w1, b1.reshape(1, -1), w_dw, b_dw.reshape(1, -1), ln_g.reshape(1, -1),
      ln_b.reshape(1, -1), w2, b2.reshape(1, -1))


def _ffn_kernel(x_ref, mod_ref, ng_ref, wg_ref, wu_ref, wd_ref, out_ref, h_scr, acc_scr):
    j = pl.program_id(1)

    @pl.when(j == 0)
    def _():
        _modnorm_to(x_ref, ng_ref, 2, mod_ref, 3, 4, h_scr)
        acc_scr[...] = jnp.zeros_like(acc_scr)

    h = h_scr[...]
    gate = jnp.dot(h, wg_ref[...], preferred_element_type=F32)
    up = jnp.dot(h, wu_ref[...], preferred_element_type=F32)
    a = (jax.nn.silu(gate) * up).astype(BF16)
    acc_scr[...] += jnp.dot(a, wd_ref[...], preferred_element_type=F32)

    @pl.when(j == pl.num_programs(1) - 1)
    def _():
        _residual_to(x_ref, acc_scr, ng_ref, 3, mod_ref, 5, out_ref)


def _ffn(x2, mod, norm_g, layer, w_gu, w_down, seq):
    rows = x2.shape[0]
    tm, tf = TM_FFN, TF_FFN
    n_f = D_FF // tf
    return pl.pallas_call(
        _ffn_kernel,
        grid=(rows // tm, n_f),
        in_specs=[
            pl.BlockSpec((tm, D_MODEL), lambda i, j: (i, 0)),
            _mod_spec(layer, seq // tm),
            _ng_spec(layer),
            pl.BlockSpec((D_MODEL, tf), lambda i, j: (0, j)),
            pl.BlockSpec((D_MODEL, tf), lambda i, j: (0, j + n_f)),
            pl.BlockSpec((tf, D_MODEL), lambda i, j: (j, 0)),
        ],
        out_specs=pl.BlockSpec((tm, D_MODEL), lambda i, j: (i, 0)),
        out_shape=jax.ShapeDtypeStruct((rows, D_MODEL), F32),
        scratch_shapes=[pltpu.VMEM((tm, D_MODEL), BF16), pltpu.VMEM((tm, D_MODEL), F32)],
        compiler_params=_params("arbitrary", "arbitrary"),
        name="ffn",
    )(x2, mod, norm_g, w_gu, w_gu, w_down)


def kernel(x, c, w_mod, b_mod, norm_g, w_in_ab, w_s, b_s, ln_v_g, ln_v_b, w_out_ab, w_pw1, b_pw1, w_dw,
           b_dw, ln_c_g, ln_c_b, w_pw2, b_pw2, w_gate_up, w_down):
    bsz, seq, _ = x.shape
    depth = w_mod.shape[0]
    mod = _modulation(c, w_mod, b_mod).reshape(depth, bsz, 6, D_MODEL)
    x2 = x.reshape(bsz * seq, D_MODEL)
    for layer in range(depth):
        if layer % 2 == 0:
            e = layer // 2
            z = _in_proj(x2, mod, norm_g, layer, w_in_ab[e].astype(BF16), ln_v_g[e], ln_v_b[e], seq)
            a = _attention(z, bsz, seq)
            b_s_wide = jnp.repeat(b_s[e], GROUP_DIM, axis=1)
            x2 = _out_proj(x2, mod, norm_g, layer, a, z, w_s[e], b_s_wide, w_out_ab[e].astype(BF16), seq)
        else:
            o = layer // 2
            x2 = _conformer(x2, mod, norm_g, layer, w_pw1[o].astype(BF16), b_pw1[o], w_dw[o], b_dw[o],
                            ln_c_g[o], ln_c_b[o], w_pw2[o].astype(BF16), b_pw2[o], seq)
        x2 = _ffn(x2, mod, norm_g, layer, w_gate_up[layer].astype(BF16), w_down[layer].astype(BF16), seq)
    return x2.reshape(bsz, seq, D_MODEL)
```

```python
import functools

import jax
import jax.numpy as jnp
from jax import lax
from jax.experimental import pallas as pl
from jax.experimental.pallas import tpu as pltpu

F32 = jnp.float32
BF16 = jnp.bfloat16

D_MODEL = 2048
N_HEADS = 8
HEAD_DIM = 128
WIDTH_A = N_HEADS * HEAD_DIM
N_GROUPS = 8
GROUP_DIM = 128
WIDTH_B = N_GROUPS * GROUP_DIM
CHUNK = 128
ATTN_BLOCK = 128
DILATIONS = (1, 4, 16)
AB_IN = 3 * WIDTH_A + 2 * WIDTH_B
CONV_WIDTH = 31
CONV_HALO = 32
D_FF = 5632
EPS = 1e-6
LANES = 128

VMEM_LIMIT_BYTES = 56 * 1024 * 1024

TM_IN = 256
TM_OUT = 256
TM_CONF = 256
CONF_TN = 256
W_SLAB = 512
TM_FFN = 512
TF_FFN = 512
TN_MOD = 1024
ROW_CHUNK = 16
ROW_UNROLL = 4
ATTN_UNROLL = 8


def _params(*sem):
    return pltpu.CompilerParams(dimension_semantics=sem, vmem_limit_bytes=VMEM_LIMIT_BYTES)


def _resident(shape):
    return pl.BlockSpec(shape, lambda *_: (0,) * len(shape), pipeline_mode=pl.Buffered(1))


def _slabs(w):
    k, n = w.shape
    return w.reshape(k, n // W_SLAB, W_SLAB).transpose(1, 0, 2).astype(BF16)


def _mod_spec(layer, tpb):
    return pl.BlockSpec((None, None, 6, D_MODEL), lambda i, *_: (layer, i // tpb, 0, 0))


def _ng_spec(layer):
    return pl.BlockSpec((None, 4, D_MODEL), lambda *_: (layer, 0, 0))


def _layer_norm(x, g, b):
    mu = jnp.mean(x, axis=-1, keepdims=True)
    xc = x - mu
    var = jnp.mean(xc * xc, axis=-1, keepdims=True)
    return xc * lax.rsqrt(var + EPS) * g + b


def _modnorm_to(x_ref, ng_ref, g_row, mod_ref, shift_row, scale_row, h_ref, zero_ref=None):
    def body(c, carry):
        rows = pl.ds(pl.multiple_of(c * ROW_CHUNK, ROW_CHUNK), ROW_CHUNK)
        if zero_ref is not None:
            zero_ref[rows, :] = jnp.zeros((ROW_CHUNK, zero_ref.shape[1]), zero_ref.dtype)
        x = x_ref[rows, :]
        rinv = lax.rsqrt(jnp.mean(x * x, axis=-1, keepdims=True) + EPS)
        gs = ng_ref[g_row:g_row + 1, :] * (1.0 + mod_ref[scale_row:scale_row + 1, :])
        h_ref[rows, :] = (x * rinv * gs + mod_ref[shift_row:shift_row + 1, :]).astype(BF16)
        return carry

    lax.fori_loop(0, x_ref.shape[0] // ROW_CHUNK, body, 0, unroll=ROW_UNROLL)


def _residual_to(x_ref, y_ref, ng_ref, g_row, mod_ref, gate_row, out_ref):
    def body(c, carry):
        rows = pl.ds(pl.multiple_of(c * ROW_CHUNK, ROW_CHUNK), ROW_CHUNK)
        y = y_ref[rows, :]
        rinv = lax.rsqrt(jnp.mean(y * y, axis=-1, keepdims=True) + EPS)
        gg = ng_ref[g_row:g_row + 1, :] * mod_ref[gate_row:gate_row + 1, :]
        out_ref[rows, :] = x_ref[rows, :] + y * rinv * gg
        return carry

    lax.fori_loop(0, x_ref.shape[0] // ROW_CHUNK, body, 0, unroll=ROW_UNROLL)


def _mod_kernel(c_ref, w_ref, b_ref, o_ref):
    c_act = jax.nn.silu(c_ref[...]).astype(BF16)
    o_ref[...] = jnp.dot(c_act, w_ref[...].astype(BF16), preferred_element_type=F32) + b_ref[...]


def _modulation(c, w_mod, b_mod):
    bsz = c.shape[0]
    depth, _, n_out = w_mod.shape
    return pl.pallas_call(
        _mod_kernel,
        grid=(depth, n_out // TN_MOD),
        in_specs=[
            pl.BlockSpec((bsz, D_MODEL), lambda l, j: (0, 0)),
            pl.BlockSpec((None, D_MODEL, TN_MOD), lambda l, j: (l, 0, j)),
            pl.BlockSpec((None, 1, TN_MOD), lambda l, j: (l, 0, j)),
        ],
        out_specs=pl.BlockSpec((None, bsz, TN_MOD), lambda l, j: (l, 0, j)),
        out_shape=jax.ShapeDtypeStruct((depth, bsz, n_out), F32),
        compiler_params=_params("arbitrary", "arbitrary"),
        name="modulation",
    )(c, w_mod, b_mod.reshape(depth, 1, n_out))


def _in_proj_kernel(x_ref, mod_ref, ng_ref, w_ref, lng_ref, lnb_ref, z_ref, h_scr):
    _modnorm_to(x_ref, ng_ref, 0, mod_ref, 0, 1, h_scr)
    h = h_scr[...]
    per = WIDTH_A // W_SLAB
    for j in range(AB_IN // WIDTH_A):
        cols = slice(j * WIDTH_A, (j + 1) * WIDTH_A)
        z = jnp.concatenate(
            [jnp.dot(h, w_ref[j * per + n], preferred_element_type=F32) for n in range(per)], axis=1)
        if j >= 3:
            z = jax.nn.gelu(z)
        if j == 4:
            z = _layer_norm(z, lng_ref[...], lnb_ref[...])
        z_ref[:, cols] = z


def _in_proj(x2, mod, norm_g, layer, w_in, ln_g, ln_b, seq):
    rows = x2.shape[0]
    tm = TM_IN
    return pl.pallas_call(
        _in_proj_kernel,
        grid=(rows // tm,),
        in_specs=[
            pl.BlockSpec((tm, D_MODEL), lambda i: (i, 0)),
            _mod_spec(layer, seq // tm),
            _ng_spec(layer),
            _resident((AB_IN // W_SLAB, D_MODEL, W_SLAB)),
            pl.BlockSpec((1, WIDTH_B), lambda i: (0, 0)),
            pl.BlockSpec((1, WIDTH_B), lambda i: (0, 0)),
        ],
        out_specs=pl.BlockSpec((tm, AB_IN), lambda i: (i, 0)),
        out_shape=jax.ShapeDtypeStruct((rows, AB_IN), F32),
        scratch_shapes=[pltpu.VMEM((tm, D_MODEL), BF16)],
        compiler_params=_params("arbitrary"),
        name="in_proj",
    )(x2, mod, norm_g, w_in, ln_g.reshape(1, WIDTH_B), ln_b.reshape(1, WIDTH_B))


def _rows(start, size, stride):
    return pl.ds(start, size) if stride == 1 else pl.ds(start, size, stride=stride)


def _attn_kernel(q_ref, k_ref, v_ref, a_ref, k_pad, v_pad, o1, o2, o3, l1, l2, l3):
    seq = q_ref.shape[0]
    blk = ATTN_BLOCK
    pad = blk * max(DILATIONS)
    k_pad[0:pad, :] = jnp.zeros((pad, HEAD_DIM), F32)
    v_pad[0:pad, :] = jnp.zeros((pad, HEAD_DIM), F32)
    k_pad[pad:, :] = k_ref[...]
    v_pad[pad:, :] = v_ref[...]

    qi = lax.broadcasted_iota(jnp.int32, (blk, 2 * blk), 0)
    kj = lax.broadcasted_iota(jnp.int32, (blk, 2 * blk), 1)
    band = (kj >= qi) & (kj <= qi + blk)
    scale = HEAD_DIM ** -0.5

    for d, o_scr, l_scr in zip(DILATIONS, (o1, o2, o3), (l1, l2, l3)):
        per_class = seq // (blk * d)
        shift = per_class.bit_length() - 1

        def body(idx, carry, d=d, o_scr=o_scr, l_scr=l_scr, per_class=per_class, shift=shift):
            r = lax.shift_right_logical(idx, shift)
            m = lax.bitwise_and(idx, per_class - 1)
            start = r + m * (blk * d)
            q = q_ref[_rows(start, blk, d), :].astype(BF16)
            kk = k_pad[_rows(pad + start - blk * d, 2 * blk, d), :].astype(BF16)
            vv = v_pad[_rows(pad + start - blk * d, 2 * blk, d), :].astype(BF16)
            s = lax.dot_general(q, kk, (((1,), (1,)), ((), ())), preferred_element_type=F32) * scale
            first_key = jnp.where(m == 0, blk, 0)
            s = jnp.where(band & (kj >= first_key), s, -jnp.inf)
            mx = jnp.max(s, axis=-1, keepdims=True)
            p = jnp.exp(s - mx)
            den = jnp.sum(p, axis=-1, keepdims=True)
            o = jnp.dot(p.astype(BF16), vv, preferred_element_type=F32) / den
            o_scr[_rows(start, blk, d), :] = o
            l_scr[_rows(start, blk, d), :] = jnp.broadcast_to(mx + jnp.log(den), (blk, HEAD_DIM))
            return carry

        lax.fori_loop(0, seq // blk, body, 0, unroll=ATTN_UNROLL)

    def mix(c, carry):
        rows = pl.ds(pl.multiple_of(c * blk, blk), blk)
        la, lb, lc = l1[rows, :], l2[rows, :], l3[rows, :]
        lm = jnp.maximum(jnp.maximum(la, lb), lc)
        ea, eb, ec = jnp.exp(la - lm), jnp.exp(lb - lm), jnp.exp(lc - lm)
        esum = ea + eb + ec
        out = (ea / esum) * o1[rows, :] + (eb / esum) * o2[rows, :] + (ec / esum) * o3[rows, :]
        a_ref[rows, :] = out.astype(BF16)
        return carry

    lax.fori_loop(0, seq // blk, mix, 0)


def _attention(z, bsz, seq):
    z3 = z.reshape(bsz, seq, AB_IN)
    pad = ATTN_BLOCK * max(DILATIONS)
    head = lambda part: pl.BlockSpec((None, seq, HEAD_DIM), lambda b, h: (b, 0, part * N_HEADS + h))
    a = pl.pallas_call(
        _attn_kernel,
        grid=(bsz, N_HEADS),
        in_specs=[head(0), head(1), head(2)],
        out_specs=pl.BlockSpec((None, seq, HEAD_DIM), lambda b, h: (b, 0, h)),
        out_shape=jax.ShapeDtypeStruct((bsz, seq, WIDTH_A), BF16),
        scratch_shapes=[pltpu.VMEM((pad + seq, HEAD_DIM), F32)] * 2 + [pltpu.VMEM((seq, HEAD_DIM), F32)] * 6,
        compiler_params=_params("arbitrary", "arbitrary"),
        name="attention",
    )(z3, z3, z3)
    return a.reshape(bsz * seq, WIDTH_A)


def _out_proj_kernel(x_ref, mod_ref, ng_ref, a_ref, u_ref, g_ref, ws_ref, bs_ref, w_ref, out_ref,
                     cat_scr, y_scr):
    tm = x_ref.shape[0]
    n_chunk = tm // CHUNK
    cat_scr[:, 0:WIDTH_A] = a_ref[...]

    ci = lax.broadcasted_iota(jnp.int32, (CHUNK, CHUNK), 0)
    cj = lax.broadcasted_iota(jnp.int32, (CHUNK, CHUNK), 1)
    tril = cj <= ci
    for h in range(N_GROUPS):
        hs = slice(h * GROUP_DIM, (h + 1) * GROUP_DIM)
        w_c = jnp.where(tril, ws_ref[h], 0.0).astype(BF16)
        g_side = jnp.concatenate(
            [g_ref[c * CHUNK:(c + 1) * CHUNK, hs] for c in range(n_chunk)], axis=1).astype(BF16)
        sp = jnp.dot(w_c, g_side, preferred_element_type=F32)
        for c in range(n_chunk):
            rs = slice(c * CHUNK, (c + 1) * CHUNK)
            sp_c = sp[:, c * GROUP_DIM:(c + 1) * GROUP_DIM] + bs_ref[:, hs]
            cat_scr[rs, WIDTH_A + h * GROUP_DIM:WIDTH_A + (h + 1) * GROUP_DIM] = (
                u_ref[rs, hs] * sp_c).astype(BF16)

    for n in range(D_MODEL // W_SLAB):
        y_scr[:, n * W_SLAB:(n + 1) * W_SLAB] = jnp.dot(cat_scr[...], w_ref[n], preferred_element_type=F32)
    _residual_to(x_ref, y_scr, ng_ref, 1, mod_ref, 2, out_ref)


def _out_proj(x2, mod, norm_g, layer, a, z, w_s, b_s_wide, w_out, seq):
    rows = x2.shape[0]
    tm = TM_OUT
    return pl.pallas_call(
        _out_proj_kernel,
        grid=(rows // tm,),
        in_specs=[
            pl.BlockSpec((tm, D_MODEL), lambda i: (i, 0)),
            _mod_spec(layer, seq // tm),
            _ng_spec(layer),
            pl.BlockSpec((tm, WIDTH_A), lambda i: (i, 0)),
            pl.BlockSpec((tm, WIDTH_B), lambda i: (i, 3)),
            pl.BlockSpec((tm, WIDTH_B), lambda i: (i, 4)),
            pl.BlockSpec((N_GROUPS, CHUNK, CHUNK), lambda i: (0, 0, 0)),
            pl.BlockSpec((CHUNK, WIDTH_B), lambda i: (0, 0)),
            _resident((D_MODEL // W_SLAB, WIDTH_A + WIDTH_B, W_SLAB)),
        ],
        out_specs=pl.BlockSpec((tm, D_MODEL), lambda i: (i, 0)),
        out_shape=jax.ShapeDtypeStruct((rows, D_MODEL), F32),
        scratch_shapes=[pltpu.VMEM((tm, WIDTH_A + WIDTH_B), BF16), pltpu.VMEM((tm, D_MODEL), F32)],
        compiler_params=_params("arbitrary"),
        name="out_proj",
    )(x2, mod, norm_g, a, z, z, w_s, b_s_wide, w_out)


def _conformer_kernel(x_ref, mod_ref, ng_ref, w1_ref, b1_ref, wdw_ref, bdw_ref, lng_ref, lnb_ref,
                      w2_ref, b2_ref, out_ref, h_scr, glu_scr, conv_scr, y_scr, *, tiles_per_seq):
    tm = x_ref.shape[0]
    n_slab = D_MODEL // LANES

    @pl.when(pl.program_id(0) % tiles_per_seq == 0)
    def _():
        glu_scr[:, 0:CONV_HALO, :] = jnp.zeros((n_slab, CONV_HALO, LANES), F32)

    _modnorm_to(x_ref, ng_ref, 0, mod_ref, 0, 1, h_scr)
    base = CONV_HALO - (CONV_WIDTH - 1)
    slabs_per = CONF_TN // LANES
    n_col = D_MODEL // CONF_TN

    def glu_chunk(jj):
        a_cols = slice(jj * CONF_TN, (jj + 1) * CONF_TN)
        b_cols = slice(D_MODEL + jj * CONF_TN, D_MODEL + (jj + 1) * CONF_TN)
        h = h_scr[...]
        za = jnp.dot(h, w1_ref[:, a_cols], preferred_element_type=F32) + b1_ref[:, a_cols]
        zb = jnp.dot(h, w1_ref[:, b_cols], preferred_element_type=F32) + b1_ref[:, b_cols]
        glu = za * jax.nn.sigmoid(zb)
        for s in range(slabs_per):
            glu_scr[jj * slabs_per + s, CONV_HALO:, :] = glu[:, s * LANES:(s + 1) * LANES]

    def conv_chunk(jj):
        for s in range(slabs_per):
            j = jj * slabs_per + s
            cols = slice(j * LANES, (j + 1) * LANES)
            acc = jnp.broadcast_to(bdw_ref[:, cols], (tm, LANES))
            for k in range(CONV_WIDTH):
                acc = acc + wdw_ref[k:k + 1, cols] * glu_scr[j, pl.ds(base + k, tm), :]
            conv_scr[:, cols] = acc
            glu_scr[j, 0:CONV_HALO, :] = glu_scr[j, tm:tm + CONV_HALO, :]

    glu_chunk(0)
    for jj in range(1, n_col):
        conv_chunk(jj - 1)
        glu_chunk(jj)
    conv_chunk(n_col - 1)

    def ln_body(c, carry):
        rows = pl.ds(pl.multiple_of(c * ROW_CHUNK, ROW_CHUNK), ROW_CHUNK)
        a = jax.nn.silu(_layer_norm(conv_scr[rows, :], lng_ref[...], lnb_ref[...]))
        h_scr[rows, :] = a.astype(BF16)
        return carry

    lax.fori_loop(0, tm // ROW_CHUNK, ln_body, 0, unroll=ROW_UNROLL)
    y_scr[...] = jnp.dot(h_scr[...], w2_ref[...], preferred_element_type=F32) + b2_ref[...]
    _residual_to(x_ref, y_scr, ng_ref, 1, mod_ref, 2, out_ref)


def _conformer(x2, mod, norm_g, layer, w1, b1, w_dw, b_dw, ln_g, ln_b, w2, b2, seq):
    rows = x2.shape[0]
    tm = TM_CONF
    vec = lambda n: pl.BlockSpec((1, n), lambda i: (0, 0))
    return pl.pallas_call(
        functools.partial(_conformer_kernel, tiles_per_seq=seq // tm),
        grid=(rows // tm,),
        in_specs=[
            pl.BlockSpec((tm, D_MODEL), lambda i: (i, 0)),
            _mod_spec(layer, seq // tm),
            _ng_spec(layer),
            _resident((D_MODEL, 2 * D_MODEL)),
            vec(2 * D_MODEL),
            pl.BlockSpec((CONV_WIDTH, D_MODEL), lambda i: (0, 0)),
            vec(D_MODEL), vec(D_MODEL), vec(D_MODEL),
            _resident((D_MODEL, D_MODEL)),
            vec(D_MODEL),
        ],
        out_specs=pl.BlockSpec((tm, D_MODEL), lambda i: (i, 0)),
        out_shape=jax.ShapeDtypeStruct((rows, D_MODEL), F32),
        scratch_shapes=[
            pltpu.VMEM((tm, D_MODEL), BF16),
            pltpu.VMEM((D_MODEL // LANES, CONV_HALO + tm, LANES), F32),
            pltpu.VMEM((tm, D_MODEL), F32),
            pltpu.VMEM((tm, D_MODEL), F32),
        ],
        compiler_params=_params("arbitrary"),
        name="conformer",
    )(x2, mod, norm_g, w1, b1.reshape(1, -1), w_dw, b_dw.reshape(1, -1), ln_g.reshape(1, -1),
      ln_b.reshape(1, -1), w2, b2.reshape(1, -1))


def _ffn_kernel(x_ref, mod_ref, ng_ref, wgu_ref, wd_ref, out_ref, h_scr, acc_scr):
    j = pl.program_id(1)

    @pl.when(j == 0)
    def _():
        _modnorm_to(x_ref, ng_ref, 2, mod_ref, 3, 4, h_scr, zero_ref=acc_scr)

    h = h_scr[...]
    gate = jnp.dot(h, wgu_ref[0], preferred_element_type=F32)
    up = jnp.dot(h, wgu_ref[1], preferred_element_type=F32)
    a = (jax.nn.silu(gate) * up).astype(BF16)
    acc_scr[...] += jnp.dot(a, wd_ref[...], preferred_element_type=F32)

    @pl.when(j == pl.num_programs(1) - 1)
    def _():
        _residual_to(x_ref, acc_scr, ng_ref, 3, mod_ref, 5, out_ref)


def _ffn(x2, mod, norm_g, layer, w_gu, w_down, seq):
    rows = x2.shape[0]
    tm, tf = TM_FFN, TF_FFN
    n_f = D_FF // tf
    return pl.pallas_call(
        _ffn_kernel,
        grid=(rows // tm, n_f),
        in_specs=[
            pl.BlockSpec((tm, D_MODEL), lambda i, j: (i, 0)),
            _mod_spec(layer, seq // tm),
            _ng_spec(layer),
            pl.BlockSpec((None, 2, D_MODEL, tf), lambda i, j: (j, 0, 0, 0)),
            pl.BlockSpec((tf, D_MODEL), lambda i, j: (j, 0)),
        ],
        out_specs=pl.BlockSpec((tm, D_MODEL), lambda i, j: (i, 0)),
        out_shape=jax.ShapeDtypeStruct((rows, D_MODEL), F32),
        scratch_shapes=[pltpu.VMEM((tm, D_MODEL), BF16), pltpu.VMEM((tm, D_MODEL), F32)],
        compiler_params=_params("arbitrary", "arbitrary"),
        name="ffn",
    )(x2, mod, norm_g, w_gu, w_down)


def _gate_up_tiles(w):
    n_f = D_FF // TF_FFN
    return w.reshape(D_MODEL, 2, n_f, TF_FFN).transpose(2, 1, 0, 3).astype(BF16)


def kernel(x, c, w_mod, b_mod, norm_g, w_in_ab, w_s, b_s, ln_v_g, ln_v_b, w_out_ab, w_pw1, b_pw1, w_dw,
           b_dw, ln_c_g, ln_c_b, w_pw2, b_pw2, w_gate_up, w_down):
    bsz, seq, _ = x.shape
    depth = w_mod.shape[0]
    mod = _modulation(c, w_mod, b_mod).reshape(depth, bsz, 6, D_MODEL)
    x2 = x.reshape(bsz * seq, D_MODEL)
    for layer in range(depth):
        if layer % 2 == 0:
            e = layer // 2
            z = _in_proj(x2, mod, norm_g, layer, _slabs(w_in_ab[e]), ln_v_g[e], ln_v_b[e], seq)
            a = _attention(z, bsz, seq)
            b_s_wide = jnp.repeat(b_s[e], GROUP_DIM, axis=1)
            x2 = _out_proj(x2, mod, norm_g, layer, a, z, w_s[e], b_s_wide, _slabs(w_out_ab[e]), seq)
        else:
            o = layer // 2
            x2 = _conformer(x2, mod, norm_g, layer, w_pw1[o].astype(BF16), b_pw1[o], w_dw[o], b_dw[o],
                            ln_c_g[o], ln_c_b[o], w_pw2[o].astype(BF16), b_pw2[o], seq)
        x2 = _ffn(x2, mod, norm_g, layer, _gate_up_tiles(w_gate_up[layer]), w_down[layer].astype(BF16), seq)
    return x2.reshape(bsz, seq, D_MODEL)
```

```python
import functools

import jax
import jax.numpy as jnp
from jax import lax
from jax.experimental import pallas as pl
from jax.experimental.pallas import tpu as pltpu

F32 = jnp.float32
BF16 = jnp.bfloat16

D_MODEL = 2048
N_HEADS = 8
HEAD_DIM = 128
WIDTH_A = N_HEADS * HEAD_DIM
N_GROUPS = 8
GROUP_DIM = 128
WIDTH_B = N_GROUPS * GROUP_DIM
CHUNK = 128
ATTN_BLOCK = 128
DILATIONS = (1, 4, 16)
AB_IN = 3 * WIDTH_A + 2 * WIDTH_B
CONV_WIDTH = 31
CONV_HALO = 32
D_FF = 5632
EPS = 1e-6
LANES = 128

VMEM_LIMIT_BYTES = 56 * 1024 * 1024
VMEM_LIMIT_FFN_BYTES = 60 * 1024 * 1024

TM_IN = 256
TM_OUT = 256
TM_CONF = 256
CONF_TN = 256
TM_FFN = 1024
TF_FFN = 512
TN_MOD = 1024
ROW_CHUNK = 16
ROW_UNROLL = 4
ATTN_UNROLL = 8


def _params(*sem):
    return pltpu.CompilerParams(dimension_semantics=sem, vmem_limit_bytes=VMEM_LIMIT_BYTES)


def _resident(shape):
    return pl.BlockSpec(shape, lambda *_: (0,) * len(shape), pipeline_mode=pl.Buffered(1))


def _mod_spec(layer, tpb):
    return pl.BlockSpec((None, None, 6, D_MODEL), lambda i, *_: (layer, i // tpb, 0, 0))


def _ng_spec(layer):
    return pl.BlockSpec((None, 4, D_MODEL), lambda *_: (layer, 0, 0))


def _layer_norm(x, g, b):
    mu = jnp.mean(x, axis=-1, keepdims=True)
    xc = x - mu
    var = jnp.mean(xc * xc, axis=-1, keepdims=True)
    return xc * lax.rsqrt(var + EPS) * g + b


def _modnorm_to(x_ref, ng_ref, g_row, mod_ref, shift_row, scale_row, h_ref, zero_ref=None):
    def body(c, carry):
        rows = pl.ds(pl.multiple_of(c * ROW_CHUNK, ROW_CHUNK), ROW_CHUNK)
        if zero_ref is not None:
            zero_ref[rows, :] = jnp.zeros((ROW_CHUNK, zero_ref.shape[1]), zero_ref.dtype)
        x = x_ref[rows, :]
        rinv = lax.rsqrt(jnp.mean(x * x, axis=-1, keepdims=True) + EPS)
        gs = ng_ref[g_row:g_row + 1, :] * (1.0 + mod_ref[scale_row:scale_row + 1, :])
        h_ref[rows, :] = (x * rinv * gs + mod_ref[shift_row:shift_row + 1, :]).astype(BF16)
        return carry

    lax.fori_loop(0, x_ref.shape[0] // ROW_CHUNK, body, 0, unroll=ROW_UNROLL)


def _residual_to(x_ref, y_ref, ng_ref, g_row, mod_ref, gate_row, out_ref):
    in_place = y_ref is out_ref
    chunk = ROW_CHUNK * ROW_UNROLL if in_place else ROW_CHUNK

    def body(c, carry):
        rows = pl.ds(pl.multiple_of(c * chunk, chunk), chunk)
        y = y_ref[rows, :]
        rinv = lax.rsqrt(jnp.mean(y * y, axis=-1, keepdims=True) + EPS)
        gg = ng_ref[g_row:g_row + 1, :] * mod_ref[gate_row:gate_row + 1, :]
        out_ref[rows, :] = x_ref[rows, :] + y * rinv * gg
        return carry

    lax.fori_loop(0, x_ref.shape[0] // chunk, body, 0, unroll=1 if in_place else ROW_UNROLL)


def _mod_kernel(c_ref, w_ref, b_ref, o_ref):
    c_act = jax.nn.silu(c_ref[...]).astype(BF16)
    o_ref[...] = jnp.dot(c_act, w_ref[...].astype(BF16), preferred_element_type=F32) + b_ref[...]


def _modulation(c, w_mod, b_mod):
    bsz = c.shape[0]
    depth, _, n_out = w_mod.shape
    return pl.pallas_call(
        _mod_kernel,
        grid=(depth, n_out // TN_MOD),
        in_specs=[
            pl.BlockSpec((bsz, D_MODEL), lambda l, j: (0, 0)),
            pl.BlockSpec((None, D_MODEL, TN_MOD), lambda l, j: (l, 0, j)),
            pl.BlockSpec((None, 1, TN_MOD), lambda l, j: (l, 0, j)),
        ],
        out_specs=pl.BlockSpec((None, bsz, TN_MOD), lambda l, j: (l, 0, j)),
        out_shape=jax.ShapeDtypeStruct((depth, bsz, n_out), F32),
        compiler_params=_params("arbitrary", "arbitrary"),
        name="modulation",
    )(c, w_mod, b_mod.reshape(depth, 1, n_out))


def _in_proj_kernel(x_ref, mod_ref, ng_ref, w_ref, lng_ref, lnb_ref, z_ref, h_scr):
    _modnorm_to(x_ref, ng_ref, 0, mod_ref, 0, 1, h_scr)
    h = h_scr[...]
    for j in range(AB_IN // WIDTH_A):
        cols = slice(j * WIDTH_A, (j + 1) * WIDTH_A)
        z = jnp.dot(h, w_ref[:, cols], preferred_element_type=F32)
        if j >= 3:
            z = jax.nn.gelu(z)
        if j == 4:
            z = _layer_norm(z, lng_ref[...], lnb_ref[...])
        z_ref[:, cols] = z


def _in_proj(x2, mod, norm_g, layer, w_in, ln_g, ln_b, seq):
    rows = x2.shape[0]
    tm = TM_IN
    return pl.pallas_call(
        _in_proj_kernel,
        grid=(rows // tm,),
        in_specs=[
            pl.BlockSpec((tm, D_MODEL), lambda i: (i, 0)),
            _mod_spec(layer, seq // tm),
            _ng_spec(layer),
            _resident((D_MODEL, AB_IN)),
            pl.BlockSpec((1, WIDTH_B), lambda i: (0, 0)),
            pl.BlockSpec((1, WIDTH_B), lambda i: (0, 0)),
        ],
        out_specs=pl.BlockSpec((tm, AB_IN), lambda i: (i, 0)),
        out_shape=jax.ShapeDtypeStruct((rows, AB_IN), F32),
        scratch_shapes=[pltpu.VMEM((tm, D_MODEL), BF16)],
        compiler_params=_params("arbitrary"),
        name="in_proj",
    )(x2, mod, norm_g, w_in, ln_g.reshape(1, WIDTH_B), ln_b.reshape(1, WIDTH_B))


def _rows(start, size, stride):
    return pl.ds(start, size) if stride == 1 else pl.ds(start, size, stride=stride)


def _attn_kernel(q_ref, k_ref, v_ref, a_ref, k_pad, v_pad, o1, o2, o3, l1, l2, l3):
    seq = q_ref.shape[0]
    blk = ATTN_BLOCK
    pad = blk * max(DILATIONS)
    k_pad[0:pad, :] = jnp.zeros((pad, HEAD_DIM), F32)
    v_pad[0:pad, :] = jnp.zeros((pad, HEAD_DIM), F32)
    k_pad[pad:, :] = k_ref[...]
    v_pad[pad:, :] = v_ref[...]

    qi = lax.broadcasted_iota(jnp.int32, (blk, 2 * blk), 0)
    kj = lax.broadcasted_iota(jnp.int32, (blk, 2 * blk), 1)
    band = (kj >= qi) & (kj <= qi + blk)
    scale = HEAD_DIM ** -0.5

    for d, o_scr, l_scr in zip(DILATIONS, (o1, o2, o3), (l1, l2, l3)):
        per_class = seq // (blk * d)
        shift = per_class.bit_length() - 1

        def body(idx, carry, d=d, o_scr=o_scr, l_scr=l_scr, per_class=per_class, shift=shift):
            r = lax.shift_right_logical(idx, shift)
            m = lax.bitwise_and(idx, per_class - 1)
            start = r + m * (blk * d)
            q = q_ref[_rows(start, blk, d), :].astype(BF16)
            kk = k_pad[_rows(pad + start - blk * d, 2 * blk, d), :].astype(BF16)
            vv = v_pad[_rows(pad + start - blk * d, 2 * blk, d), :].astype(BF16)
            s = lax.dot_general(q, kk, (((1,), (1,)), ((), ())), preferred_element_type=F32) * scale
            first_key = jnp.where(m == 0, blk, 0)
            s = jnp.where(band & (kj >= first_key), s, -jnp.inf)
            mx = jnp.max(s, axis=-1, keepdims=True)
            p = jnp.exp(s - mx)
            den = jnp.sum(p, axis=-1, keepdims=True)
            o = jnp.dot(p.astype(BF16), vv, preferred_element_type=F32) / den
            o_scr[_rows(start, blk, d), :] = o
            l_scr[_rows(start, blk, d), :] = jnp.broadcast_to(mx + jnp.log(den), (blk, HEAD_DIM))
            return carry

        lax.fori_loop(0, seq // blk, body, 0, unroll=ATTN_UNROLL)

    def mix(c, carry):
        rows = pl.ds(pl.multiple_of(c * blk, blk), blk)
        la, lb, lc = l1[rows, :], l2[rows, :], l3[rows, :]
        lm = jnp.maximum(jnp.maximum(la, lb), lc)
        ea, eb, ec = jnp.exp(la - lm), jnp.exp(lb - lm), jnp.exp(lc - lm)
        esum = ea + eb + ec
        out = (ea / esum) * o1[rows, :] + (eb / esum) * o2[rows, :] + (ec / esum) * o3[rows, :]
        a_ref[rows, :] = out.astype(BF16)
        return carry

    lax.fori_loop(0, seq // blk, mix, 0)


def _attention(z, bsz, seq):
    z3 = z.reshape(bsz, seq, AB_IN)
    pad = ATTN_BLOCK * max(DILATIONS)
    head = lambda part: pl.BlockSpec((None, seq, HEAD_DIM), lambda b, h: (b, 0, part * N_HEADS + h))
    a = pl.pallas_call(
        _attn_kernel,
        grid=(bsz, N_HEADS),
        in_specs=[head(0), head(1), head(2)],
        out_specs=pl.BlockSpec((None, seq, HEAD_DIM), lambda b, h: (b, 0, h)),
        out_shape=jax.ShapeDtypeStruct((bsz, seq, WIDTH_A), BF16),
        scratch_shapes=[pltpu.VMEM((pad + seq, HEAD_DIM), F32)] * 2 + [pltpu.VMEM((seq, HEAD_DIM), F32)] * 6,
        compiler_params=_params("arbitrary", "arbitrary"),
        name="attention",
    )(z3, z3, z3)
    return a.reshape(bsz * seq, WIDTH_A)


def _out_proj_kernel(x_ref, mod_ref, ng_ref, a_ref, u_ref, g_ref, ws_ref, bs_ref, w_ref, out_ref,
                     cat_scr, y_scr):
    tm = x_ref.shape[0]
    n_chunk = tm // CHUNK
    cat_scr[:, 0:WIDTH_A] = a_ref[...]

    ci = lax.broadcasted_iota(jnp.int32, (CHUNK, CHUNK), 0)
    cj = lax.broadcasted_iota(jnp.int32, (CHUNK, CHUNK), 1)
    tril = cj <= ci
    for h in range(N_GROUPS):
        hs = slice(h * GROUP_DIM, (h + 1) * GROUP_DIM)
        w_c = jnp.where(tril, ws_ref[h], 0.0).astype(BF16)
        g_side = jnp.concatenate(
            [g_ref[c * CHUNK:(c + 1) * CHUNK, hs] for c in range(n_chunk)], axis=1).astype(BF16)
        sp = jnp.dot(w_c, g_side, preferred_element_type=F32)
        for c in range(n_chunk):
            rs = slice(c * CHUNK, (c + 1) * CHUNK)
            sp_c = sp[:, c * GROUP_DIM:(c + 1) * GROUP_DIM] + bs_ref[:, hs]
            cat_scr[rs, WIDTH_A + h * GROUP_DIM:WIDTH_A + (h + 1) * GROUP_DIM] = (
                u_ref[rs, hs] * sp_c).astype(BF16)

    y_scr[...] = jnp.dot(cat_scr[...], w_ref[...], preferred_element_type=F32)
    _residual_to(x_ref, y_scr, ng_ref, 1, mod_ref, 2, out_ref)


def _out_proj(x2, mod, norm_g, layer, a, z, w_s, b_s_wide, w_out, seq):
    rows = x2.shape[0]
    tm = TM_OUT
    return pl.pallas_call(
        _out_proj_kernel,
        grid=(rows // tm,),
        in_specs=[
            pl.BlockSpec((tm, D_MODEL), lambda i: (i, 0)),
            _mod_spec(layer, seq // tm),
            _ng_spec(layer),
            pl.BlockSpec((tm, WIDTH_A), lambda i: (i, 0)),
            pl.BlockSpec((tm, WIDTH_B), lambda i: (i, 3)),
            pl.BlockSpec((tm, WIDTH_B), lambda i: (i, 4)),
            pl.BlockSpec((N_GROUPS, CHUNK, CHUNK), lambda i: (0, 0, 0)),
            pl.BlockSpec((CHUNK, WIDTH_B), lambda i: (0, 0)),
            _resident((WIDTH_A + WIDTH_B, D_MODEL)),
        ],
        out_specs=pl.BlockSpec((tm, D_MODEL), lambda i: (i, 0)),
        out_shape=jax.ShapeDtypeStruct((rows, D_MODEL), F32),
        scratch_shapes=[pltpu.VMEM((tm, WIDTH_A + WIDTH_B), BF16), pltpu.VMEM((tm, D_MODEL), F32)],
        compiler_params=_params("arbitrary"),
        name="out_proj",
    )(x2, mod, norm_g, a, z, z, w_s, b_s_wide, w_out)


def _conformer_kernel(x_ref, mod_ref, ng_ref, w1_ref, b1_ref, wdw_ref, bdw_ref, lng_ref, lnb_ref,
                      w2_ref, b2_ref, out_ref, h_scr, glu_scr, conv_scr, y_scr, *, tiles_per_seq):
    tm = x_ref.shape[0]
    n_slab = D_MODEL // LANES

    @pl.when(pl.program_id(0) % tiles_per_seq == 0)
    def _():
        glu_scr[:, 0:CONV_HALO, :] = jnp.zeros((n_slab, CONV_HALO, LANES), F32)

    _modnorm_to(x_ref, ng_ref, 0, mod_ref, 0, 1, h_scr)
    base = CONV_HALO - (CONV_WIDTH - 1)
    slabs_per = CONF_TN // LANES
    n_col = D_MODEL // CONF_TN

    def glu_chunk(jj):
        a_cols = slice(jj * CONF_TN, (jj + 1) * CONF_TN)
        b_cols = slice(D_MODEL + jj * CONF_TN, D_MODEL + (jj + 1) * CONF_TN)
        h = h_scr[...]
        za = jnp.dot(h, w1_ref[:, a_cols], preferred_element_type=F32) + b1_ref[:, a_cols]
        zb = jnp.dot(h, w1_ref[:, b_cols], preferred_element_type=F32) + b1_ref[:, b_cols]
        glu = za * jax.nn.sigmoid(zb)
        for s in range(slabs_per):
            glu_scr[jj * slabs_per + s, CONV_HALO:, :] = glu[:, s * LANES:(s + 1) * LANES]

    def conv_chunk(jj):
        for s in range(slabs_per):
            j = jj * slabs_per + s
            cols = slice(j * LANES, (j + 1) * LANES)
            acc = jnp.broadcast_to(bdw_ref[:, cols], (tm, LANES))
            for k in range(CONV_WIDTH):
                acc = acc + wdw_ref[k:k + 1, cols] * glu_scr[j, pl.ds(base + k, tm), :]
            conv_scr[:, cols] = acc
            glu_scr[j, 0:CONV_HALO, :] = glu_scr[j, tm:tm + CONV_HALO, :]

    glu_chunk(0)
    for jj in range(1, n_col):
        conv_chunk(jj - 1)
        glu_chunk(jj)
    conv_chunk(n_col - 1)

    def ln_body(c, carry):
        rows = pl.ds(pl.multiple_of(c * ROW_CHUNK, ROW_CHUNK), ROW_CHUNK)
        a = jax.nn.silu(_layer_norm(conv_scr[rows, :], lng_ref[...], lnb_ref[...]))
        h_scr[rows, :] = a.astype(BF16)
        return carry

    lax.fori_loop(0, tm // ROW_CHUNK, ln_body, 0, unroll=ROW_UNROLL)
    y_scr[...] = jnp.dot(h_scr[...], w2_ref[...], preferred_element_type=F32) + b2_ref[...]
    _residual_to(x_ref, y_scr, ng_ref, 1, mod_ref, 2, out_ref)


def _conformer(x2, mod, norm_g, layer, w1, b1, w_dw, b_dw, ln_g, ln_b, w2, b2, seq):
    rows = x2.shape[0]
    tm = TM_CONF
    vec = lambda n: pl.BlockSpec((1, n), lambda i: (0, 0))
    return pl.pallas_call(
        functools.partial(_conformer_kernel, tiles_per_seq=seq // tm),
        grid=(rows // tm,),
        in_specs=[
            pl.BlockSpec((tm, D_MODEL), lambda i: (i, 0)),
            _mod_spec(layer, seq // tm),
            _ng_spec(layer),
            _resident((D_MODEL, 2 * D_MODEL)),
            vec(2 * D_MODEL),
            pl.BlockSpec((CONV_WIDTH, D_MODEL), lambda i: (0, 0)),
            vec(D_MODEL), vec(D_MODEL), vec(D_MODEL),
            _resident((D_MODEL, D_MODEL)),
            vec(D_MODEL),
        ],
        out_specs=pl.BlockSpec((tm, D_MODEL), lambda i: (i, 0)),
        out_shape=jax.ShapeDtypeStruct((rows, D_MODEL), F32),
        scratch_shapes=[
            pltpu.VMEM((tm, D_MODEL), BF16),
            pltpu.VMEM((D_MODEL // LANES, CONV_HALO + tm, LANES), F32),
            pltpu.VMEM((tm, D_MODEL), F32),
            pltpu.VMEM((tm, D_MODEL), F32),
        ],
        compiler_params=_params("arbitrary"),
        name="conformer",
    )(x2, mod, norm_g, w1, b1.reshape(1, -1), w_dw, b_dw.reshape(1, -1), ln_g.reshape(1, -1),
      ln_b.reshape(1, -1), w2, b2.reshape(1, -1))


def _ffn_kernel(x_ref, mod_ref, ng_ref, wg_ref, wu_ref, wd_ref, out_ref, h_scr):
    j = pl.program_id(1)

    @pl.when(j == 0)
    def _():
        _modnorm_to(x_ref, ng_ref, 2, mod_ref, 3, 4, h_scr, zero_ref=out_ref)

    h = h_scr[...]
    gate = jnp.dot(h, wg_ref[...], preferred_element_type=F32)
    up = jnp.dot(h, wu_ref[...], preferred_element_type=F32)
    a = (jax.nn.silu(gate) * up).astype(BF16)
    out_ref[...] += jnp.dot(a, wd_ref[...], preferred_element_type=F32)

    @pl.when(j == pl.num_programs(1) - 1)
    def _():
        _residual_to(x_ref, out_ref, ng_ref, 3, mod_ref, 5, out_ref)


def _ffn(x2, mod, norm_g, layer, w_gu, w_down, seq):
    rows = x2.shape[0]
    tm, tf = TM_FFN, TF_FFN
    n_f = D_FF // tf
    return pl.pallas_call(
        _ffn_kernel,
        grid=(rows // tm, n_f),
        in_specs=[
            pl.BlockSpec((tm, D_MODEL), lambda i, j: (i, 0)),
            _mod_spec(layer, seq // tm),
            _ng_spec(layer),
            pl.BlockSpec((D_MODEL, tf), lambda i, j: (0, j)),
            pl.BlockSpec((D_MODEL, tf), lambda i, j: (0, j + n_f)),
            pl.BlockSpec((tf, D_MODEL), lambda i, j: (j, 0)),
        ],
        out_specs=pl.BlockSpec((tm, D_MODEL), lambda i, j: (i, 0)),
        out_shape=jax.ShapeDtypeStruct((rows, D_MODEL), F32),
        scratch_shapes=[pltpu.VMEM((tm, D_MODEL), BF16)],
        compiler_params=pltpu.CompilerParams(dimension_semantics=("arbitrary", "arbitrary"),
                                             vmem_limit_bytes=VMEM_LIMIT_FFN_BYTES),
        name="ffn",
    )(x2, mod, norm_g, w_gu, w_gu, w_down)


def kernel(x, c, w_mod, b_mod, norm_g, w_in_ab, w_s, b_s, ln_v_g, ln_v_b, w_out_ab, w_pw1, b_pw1, w_dw,
           b_dw, ln_c_g, ln_c_b, w_pw2, b_pw2, w_gate_up, w_down):
    bsz, seq, _ = x.shape
    depth = w_mod.shape[0]
    mod = _modulation(c, w_mod, b_mod).reshape(depth, bsz, 6, D_MODEL)
    x2 = x.reshape(bsz * seq, D_MODEL)
    for layer in range(depth):
        if layer % 2 == 0:
            e = layer // 2
            z = _in_proj(x2, mod, norm_g, layer, w_in_ab[e].astype(BF16), ln_v_g[e], ln_v_b[e], seq)
            a = _attention(z, bsz, seq)
            b_s_wide = jnp.repeat(b_s[e], GROUP_DIM, axis=1)
            x2 = _out_proj(x2, mod, norm_g, layer, a, z, w_s[e], b_s_wide, w_out_ab[e].astype(BF16), seq)
        else:
            o = layer // 2
            x2 = _conformer(x2, mod, norm_g, layer, w_pw1[o].astype(BF16), b_pw1[o], w_dw[o], b_dw[o],
                            ln_c_g[o], ln_c_b[o], w_pw2[o].astype(BF16), b_pw2[o], seq)
        x2 = _ffn(x2, mod, norm_g, layer, w_gate_up[layer].astype(BF16), w_down[layer].astype(BF16), seq)
    return x2.reshape(bsz, seq, D_MODEL)
```

```python
import functools

import jax
import jax.numpy as jnp
from jax import lax
from jax.experimental import pallas as pl
from jax.experimental.pallas import tpu as pltpu

F32 = jnp.float32
BF16 = jnp.bfloat16

D_MODEL = 2048
N_HEADS = 8
HEAD_DIM = 128
WIDTH_A = N_HEADS * HEAD_DIM
N_GROUPS = 8
GROUP_DIM = 128
WIDTH_B = N_GROUPS * GROUP_DIM
CHUNK = 128
ATTN_BLOCK = 128
DILATIONS = (1, 4, 16)
AB_IN = 3 * WIDTH_A + 2 * WIDTH_B
CONV_WIDTH = 31
CONV_HALO = 32
D_FF = 5632
EPS = 1e-6
LANES = 128

VMEM_LIMIT_BYTES = 56 * 1024 * 1024
VMEM_LIMIT_FFN_BYTES = 60 * 1024 * 1024

TM_IN = 256
TM_OUT = 256
TM_CONF = 256
CONF_TN = 256
TM_FFN = 1024
TF_FFN = 512
TN_MOD = 1024
ROW_CHUNK = 16
ROW_UNROLL = 16
ATTN_UNROLL = 8


def _params(*sem):
    return pltpu.CompilerParams(dimension_semantics=sem, vmem_limit_bytes=VMEM_LIMIT_BYTES)


def _resident(shape):
    return pl.BlockSpec(shape, lambda *_: (0,) * len(shape), pipeline_mode=pl.Buffered(1))


def _mod_spec(layer, tpb):
    return pl.BlockSpec((None, None, 6, D_MODEL), lambda i, *_: (layer, i // tpb, 0, 0))


def _ng_spec(layer):
    return pl.BlockSpec((None, 4, D_MODEL), lambda *_: (layer, 0, 0))


def _layer_norm(x, g, b):
    mu = jnp.mean(x, axis=-1, keepdims=True)
    xc = x - mu
    var = jnp.mean(xc * xc, axis=-1, keepdims=True)
    return xc * lax.rsqrt(var + EPS) * g + b


def _modnorm_to(x_ref, ng_ref, g_row, mod_ref, shift_row, scale_row, h_ref, zero_ref=None):
    def body(c, carry):
        rows = pl.ds(pl.multiple_of(c * ROW_CHUNK, ROW_CHUNK), ROW_CHUNK)
        if zero_ref is not None:
            zero_ref[rows, :] = jnp.zeros((ROW_CHUNK, zero_ref.shape[1]), zero_ref.dtype)
        x = x_ref[rows, :]
        rinv = lax.rsqrt(jnp.mean(x * x, axis=-1, keepdims=True) + EPS)
        gs = ng_ref[g_row:g_row + 1, :] * (1.0 + mod_ref[scale_row:scale_row + 1, :])
        h_ref[rows, :] = (x * rinv * gs + mod_ref[shift_row:shift_row + 1, :]).astype(BF16)
        return carry

    lax.fori_loop(0, x_ref.shape[0] // ROW_CHUNK, body, 0, unroll=ROW_UNROLL)


def _residual_to(x_ref, y_ref, ng_ref, g_row, mod_ref, gate_row, out_ref):
    in_place = y_ref is out_ref
    chunk = ROW_CHUNK * ROW_UNROLL if in_place else ROW_CHUNK

    def body(c, carry):
        rows = pl.ds(pl.multiple_of(c * chunk, chunk), chunk)
        y = y_ref[rows, :]
        rinv = lax.rsqrt(jnp.mean(y * y, axis=-1, keepdims=True) + EPS)
        gg = ng_ref[g_row:g_row + 1, :] * mod_ref[gate_row:gate_row + 1, :]
        out_ref[rows, :] = x_ref[rows, :] + y * rinv * gg
        return carry

    lax.fori_loop(0, x_ref.shape[0] // chunk, body, 0, unroll=1 if in_place else ROW_UNROLL)


def _mod_kernel(c_ref, w_ref, b_ref, o_ref):
    c_act = jax.nn.silu(c_ref[...]).astype(BF16)
    o_ref[...] = jnp.dot(c_act, w_ref[...].astype(BF16), preferred_element_type=F32) + b_ref[...]


def _modulation(c, w_mod, b_mod):
    bsz = c.shape[0]
    depth, _, n_out = w_mod.shape
    return pl.pallas_call(
        _mod_kernel,
        grid=(depth, n_out // TN_MOD),
        in_specs=[
            pl.BlockSpec((bsz, D_MODEL), lambda l, j: (0, 0)),
            pl.BlockSpec((None, D_MODEL, TN_MOD), lambda l, j: (l, 0, j)),
            pl.BlockSpec((None, 1, TN_MOD), lambda l, j: (l, 0, j)),
        ],
        out_specs=pl.BlockSpec((None, bsz, TN_MOD), lambda l, j: (l, 0, j)),
        out_shape=jax.ShapeDtypeStruct((depth, bsz, n_out), F32),
        compiler_params=_params("arbitrary", "arbitrary"),
        name="modulation",
    )(c, w_mod, b_mod.reshape(depth, 1, n_out))


def _in_proj_kernel(x_ref, mod_ref, ng_ref, w_ref, lng_ref, lnb_ref, z_ref, h_scr):
    _modnorm_to(x_ref, ng_ref, 0, mod_ref, 0, 1, h_scr)
    h = h_scr[...]
    for j in range(AB_IN // WIDTH_A):
        cols = slice(j * WIDTH_A, (j + 1) * WIDTH_A)
        z = jnp.dot(h, w_ref[:, cols], preferred_element_type=F32)
        if j >= 3:
            z = jax.nn.gelu(z)
        if j == 4:
            z = _layer_norm(z, lng_ref[...], lnb_ref[...])
        z_ref[:, cols] = z


def _in_proj(x2, mod, norm_g, layer, w_in, ln_g, ln_b, seq):
    rows = x2.shape[0]
    tm = TM_IN
    return pl.pallas_call(
        _in_proj_kernel,
        grid=(rows // tm,),
        in_specs=[
            pl.BlockSpec((tm, D_MODEL), lambda i: (i, 0)),
            _mod_spec(layer, seq // tm),
            _ng_spec(layer),
            _resident((D_MODEL, AB_IN)),
            pl.BlockSpec((1, WIDTH_B), lambda i: (0, 0)),
            pl.BlockSpec((1, WIDTH_B), lambda i: (0, 0)),
        ],
        out_specs=pl.BlockSpec((tm, AB_IN), lambda i: (i, 0)),
        out_shape=jax.ShapeDtypeStruct((rows, AB_IN), F32),
        scratch_shapes=[pltpu.VMEM((tm, D_MODEL), BF16)],
        compiler_params=_params("arbitrary"),
        name="in_proj",
    )(x2, mod, norm_g, w_in, ln_g.reshape(1, WIDTH_B), ln_b.reshape(1, WIDTH_B))


def _rows(start, size, stride):
    return pl.ds(start, size) if stride == 1 else pl.ds(start, size, stride=stride)


def _attn_kernel(q_ref, k_ref, v_ref, a_ref, k_pad, v_pad, o1, o2, o3, l1, l2, l3):
    seq = q_ref.shape[0]
    blk = ATTN_BLOCK
    pad = blk * max(DILATIONS)
    k_pad[0:pad, :] = jnp.zeros((pad, HEAD_DIM), F32)
    v_pad[0:pad, :] = jnp.zeros((pad, HEAD_DIM), F32)
    k_pad[pad:, :] = k_ref[...]
    v_pad[pad:, :] = v_ref[...]

    qi = lax.broadcasted_iota(jnp.int32, (blk, 2 * blk), 0)
    kj = lax.broadcasted_iota(jnp.int32, (blk, 2 * blk), 1)
    band = (kj >= qi) & (kj <= qi + blk)
    scale = HEAD_DIM ** -0.5

    for d, o_scr, l_scr in zip(DILATIONS, (o1, o2, o3), (l1, l2, l3)):
        per_class = seq // (blk * d)
        shift = per_class.bit_length() - 1

        def body(idx, carry, d=d, o_scr=o_scr, l_scr=l_scr, per_class=per_class, shift=shift):
            r = lax.shift_right_logical(idx, shift)
            m = lax.bitwise_and(idx, per_class - 1)
            start = r + m * (blk * d)
            q = q_ref[_rows(start, blk, d), :].astype(BF16)
            kk = k_pad[_rows(pad + start - blk * d, 2 * blk, d), :].astype(BF16)
            vv = v_pad[_rows(pad + start - blk * d, 2 * blk, d), :].astype(BF16)
            s = lax.dot_general(q, kk, (((1,), (1,)), ((), ())), preferred_element_type=F32) * scale
            first_key = jnp.where(m == 0, blk, 0)
            s = jnp.where(band & (kj >= first_key), s, -jnp.inf)
            mx = jnp.max(s, axis=-1, keepdims=True)
            p = jnp.exp(s - mx)
            den = jnp.sum(p, axis=-1, keepdims=True)
            o = jnp.dot(p.astype(BF16), vv, preferred_element_type=F32) / den
            o_scr[_rows(start, blk, d), :] = o
            l_scr[_rows(start, blk, d), :] = jnp.broadcast_to(mx + jnp.log(den), (blk, HEAD_DIM))
            return carry

        lax.fori_loop(0, seq // blk, body, 0, unroll=ATTN_UNROLL)

    def mix(c, carry):
        rows = pl.ds(pl.multiple_of(c * blk, blk), blk)
        la, lb, lc = l1[rows, :], l2[rows, :], l3[rows, :]
        lm = jnp.maximum(jnp.maximum(la, lb), lc)
        ea, eb, ec = jnp.exp(la - lm), jnp.exp(lb - lm), jnp.exp(lc - lm)
        esum = ea + eb + ec
        out = (ea / esum) * o1[rows, :] + (eb / esum) * o2[rows, :] + (ec / esum) * o3[rows, :]
        a_ref[rows, :] = out.astype(BF16)
        return carry

    lax.fori_loop(0, seq // blk, mix, 0)


def _attention(z, bsz, seq):
    z3 = z.reshape(bsz, seq, AB_IN)
    pad = ATTN_BLOCK * max(DILATIONS)
    head = lambda part: pl.BlockSpec((None, seq, HEAD_DIM), lambda b, h: (b, 0, part * N_HEADS + h))
    a = pl.pallas_call(
        _attn_kernel,
        grid=(bsz, N_HEADS),
        in_specs=[head(0), head(1), head(2)],
        out_specs=pl.BlockSpec((None, seq, HEAD_DIM), lambda b, h: (b, 0, h)),
        out_shape=jax.ShapeDtypeStruct((bsz, seq, WIDTH_A), BF16),
        scratch_shapes=[pltpu.VMEM((pad + seq, HEAD_DIM), F32)] * 2 + [pltpu.VMEM((seq, HEAD_DIM), F32)] * 6,
        compiler_params=_params("arbitrary", "arbitrary"),
        name="attention",
    )(z3, z3, z3)
    return a.reshape(bsz * seq, WIDTH_A)


def _out_proj_kernel(x_ref, mod_ref, ng_ref, a_ref, u_ref, g_ref, ws_ref, bs_ref, w_ref, out_ref,
                     cat_scr, y_scr):
    tm = x_ref.shape[0]
    n_chunk = tm // CHUNK
    cat_scr[:, 0:WIDTH_A] = a_ref[...]

    ci = lax.broadcasted_iota(jnp.int32, (CHUNK, CHUNK), 0)
    cj = lax.broadcasted_iota(jnp.int32, (CHUNK, CHUNK), 1)
    tril = cj <= ci
    for h in range(N_GROUPS):
        hs = slice(h * GROUP_DIM, (h + 1) * GROUP_DIM)
        w_c = jnp.where(tril, ws_ref[h], 0.0).astype(BF16)
        g_side = jnp.concatenate(
            [g_ref[c * CHUNK:(c + 1) * CHUNK, hs] for c in range(n_chunk)], axis=1).astype(BF16)
        sp = jnp.dot(w_c, g_side, preferred_element_type=F32)
        for c in range(n_chunk):
            rs = slice(c * CHUNK, (c + 1) * CHUNK)
            sp_c = sp[:, c * GROUP_DIM:(c + 1) * GROUP_DIM] + bs_ref[:, hs]
            cat_scr[rs, WIDTH_A + h * GROUP_DIM:WIDTH_A + (h + 1) * GROUP_DIM] = (
                u_ref[rs, hs] * sp_c).astype(BF16)

    y_scr[...] = jnp.dot(cat_scr[...], w_ref[...], preferred_element_type=F32)
    _residual_to(x_ref, y_scr, ng_ref, 1, mod_ref, 2, out_ref)


def _out_proj(x2, mod, norm_g, layer, a, z, w_s, b_s_wide, w_out, seq):
    rows = x2.shape[0]
    tm = TM_OUT
    return pl.pallas_call(
        _out_proj_kernel,
        grid=(rows // tm,),
        in_specs=[
            pl.BlockSpec((tm, D_MODEL), lambda i: (i, 0)),
            _mod_spec(layer, seq // tm),
            _ng_spec(layer),
            pl.BlockSpec((tm, WIDTH_A), lambda i: (i, 0)),
            pl.BlockSpec((tm, WIDTH_B), lambda i: (i, 3)),
            pl.BlockSpec((tm, WIDTH_B), lambda i: (i, 4)),
            pl.BlockSpec((N_GROUPS, CHUNK, CHUNK), lambda i: (0, 0, 0)),
            pl.BlockSpec((CHUNK, WIDTH_B), lambda i: (0, 0)),
            _resident((WIDTH_A + WIDTH_B, D_MODEL)),
        ],
        out_specs=pl.BlockSpec((tm, D_MODEL), lambda i: (i, 0)),
        out_shape=jax.ShapeDtypeStruct((rows, D_MODEL), F32),
        scratch_shapes=[pltpu.VMEM((tm, WIDTH_A + WIDTH_B), BF16), pltpu.VMEM((tm, D_MODEL), F32)],
        compiler_params=_params("arbitrary"),
        name="out_proj",
    )(x2, mod, norm_g, a, z, z, w_s, b_s_wide, w_out)


def _conformer_kernel(x_ref, mod_ref, ng_ref, w1_ref, b1_ref, wdw_ref, bdw_ref, lng_ref, lnb_ref,
                      w2_ref, b2_ref, out_ref, h_scr, glu_scr, conv_scr, y_scr, *, tiles_per_seq):
    tm = x_ref.shape[0]
    n_slab = D_MODEL // LANES

    @pl.when(pl.program_id(0) % tiles_per_seq == 0)
    def _():
        glu_scr[:, 0:CONV_HALO, :] = jnp.zeros((n_slab, CONV_HALO, LANES), F32)

    _modnorm_to(x_ref, ng_ref, 0, mod_ref, 0, 1, h_scr)
    base = CONV_HALO - (CONV_WIDTH - 1)
    slabs_per = CONF_TN // LANES
    n_col = D_MODEL // CONF_TN

    def glu_chunk(jj):
        a_cols = slice(jj * CONF_TN, (jj + 1) * CONF_TN)
        b_cols = slice(D_MODEL + jj * CONF_TN, D_MODEL + (jj + 1) * CONF_TN)
        h = h_scr[...]
        za = jnp.dot(h, w1_ref[:, a_cols], preferred_element_type=F32) + b1_ref[:, a_cols]
        zb = jnp.dot(h, w1_ref[:, b_cols], preferred_element_type=F32) + b1_ref[:, b_cols]
        glu = za * jax.nn.sigmoid(zb)
        for s in range(slabs_per):
            glu_scr[jj * slabs_per + s, CONV_HALO:, :] = glu[:, s * LANES:(s + 1) * LANES]

    def conv_chunk(jj):
        for s in range(slabs_per):
            j = jj * slabs_per + s
            cols = slice(j * LANES, (j + 1) * LANES)
            acc = jnp.broadcast_to(bdw_ref[:, cols], (tm, LANES))
            for k in range(CONV_WIDTH):
                acc = acc + wdw_ref[k:k + 1, cols] * glu_scr[j, pl.ds(base + k, tm), :]
            conv_scr[:, cols] = acc
            glu_scr[j, 0:CONV_HALO, :] = glu_scr[j, tm:tm + CONV_HALO, :]

    glu_chunk(0)
    for jj in range(1, n_col):
        conv_chunk(jj - 1)
        glu_chunk(jj)
    conv_chunk(n_col - 1)

    def ln_body(c, carry):
        rows = pl.ds(pl.multiple_of(c * ROW_CHUNK, ROW_CHUNK), ROW_CHUNK)
        a = jax.nn.silu(_layer_norm(conv_scr[rows, :], lng_ref[...], lnb_ref[...]))
        h_scr[rows, :] = a.astype(BF16)
        return carry

    lax.fori_loop(0, tm // ROW_CHUNK, ln_body, 0, unroll=ROW_UNROLL)
    y_scr[...] = jnp.dot(h_scr[...], w2_ref[...], preferred_element_type=F32) + b2_ref[...]
    _residual_to(x_ref, y_scr, ng_ref, 1, mod_ref, 2, out_ref)


def _conformer(x2, mod, norm_g, layer, w1, b1, w_dw, b_dw, ln_g, ln_b, w2, b2, seq):
    rows = x2.shape[0]
    tm = TM_CONF
    vec = lambda n: pl.BlockSpec((1, n), lambda i: (0, 0))
    return pl.pallas_call(
        functools.partial(_conformer_kernel, tiles_per_seq=seq // tm),
        grid=(rows // tm,),
        in_specs=[
            pl.BlockSpec((tm, D_MODEL), lambda i: (i, 0)),
            _mod_spec(layer, seq // tm),
            _ng_spec(layer),
            _resident((D_MODEL, 2 * D_MODEL)),
            vec(2 * D_MODEL),
            pl.BlockSpec((CONV_WIDTH, D_MODEL), lambda i: (0, 0)),
            vec(D_MODEL), vec(D_MODEL), vec(D_MODEL),
            _resident((D_MODEL, D_MODEL)),
            vec(D_MODEL),
        ],
        out_specs=pl.BlockSpec((tm, D_MODEL), lambda i: (i, 0)),
        out_shape=jax.ShapeDtypeStruct((rows, D_MODEL), F32),
        scratch_shapes=[
            pltpu.VMEM((tm, D_MODEL), BF16),
            pltpu.VMEM((D_MODEL // LANES, CONV_HALO + tm, LANES), F32),
            pltpu.VMEM((tm, D_MODEL), F32),
            pltpu.VMEM((tm, D_MODEL), F32),
        ],
        compiler_params=_params("arbitrary"),
        name="conformer",
    )(x2, mod, norm_g, w1, b1.reshape(1, -1), w_dw, b_dw.reshape(1, -1), ln_g.reshape(1, -1),
      ln_b.reshape(1, -1), w2, b2.reshape(1, -1))


def _ffn_kernel(x_ref, mod_ref, ng_ref, wg_ref, wu_ref, wd_ref, out_ref, h_scr):
    j = pl.program_id(1)

    @pl.when(j == 0)
    def _():
        _modnorm_to(x_ref, ng_ref, 2, mod_ref, 3, 4, h_scr, zero_ref=out_ref)

    h = h_scr[...]
    gate = jnp.dot(h, wg_ref[...], preferred_element_type=F32)
    up = jnp.dot(h, wu_ref[...], preferred_element_type=F32)
    a = (jax.nn.silu(gate) * up).astype(BF16)
    out_ref[...] += jnp.dot(a, wd_ref[...], preferred_element_type=F32)

    @pl.when(j == pl.num_programs(1) - 1)
    def _():
        _residual_to(x_ref, out_ref, ng_ref, 3, mod_ref, 5, out_ref)


def _ffn(x2, mod, norm_g, layer, w_gu, w_down, seq):
    rows = x2.shape[0]
    tm, tf = TM_FFN, TF_FFN
    n_f = D_FF // tf
    return pl.pallas_call(
        _ffn_kernel,
        grid=(rows // tm, n_f),
        in_specs=[
            pl.BlockSpec((tm, D_MODEL), lambda i, j: (i, 0)),
            _mod_spec(layer, seq // tm),
            _ng_spec(layer),
            pl.BlockSpec((D_MODEL, tf), lambda i, j: (0, j)),
            pl.BlockSpec((D_MODEL, tf), lambda i, j: (0, j + n_f)),
            pl.BlockSpec((tf, D_MODEL), lambda i, j: (j, 0)),
        ],
        out_specs=pl.BlockSpec((tm, D_MODEL), lambda i, j: (i, 0)),
        out_shape=jax.ShapeDtypeStruct((rows, D_MODEL), F32),
        scratch_shapes=[pltpu.VMEM((tm, D_MODEL), BF16)],
        compiler_params=pltpu.CompilerParams(dimension_semantics=("arbitrary", "arbitrary"),
                                             vmem_limit_bytes=VMEM_LIMIT_FFN_BYTES),
        name="ffn",
    )(x2, mod, norm_g, w_gu, w_gu, w_down)


def kernel(x, c, w_mod, b_mod, norm_g, w_in_ab, w_s, b_s, ln_v_g, ln_v_b, w_out_ab, w_pw1, b_pw1, w_dw,
           b_dw, ln_c_g, ln_c_b, w_pw2, b_pw2, w_gate_up, w_down):
    bsz, seq, _ = x.shape
    depth = w_mod.shape[0]
    mod = _modulation(c, w_mod, b_mod).reshape(depth, bsz, 6, D_MODEL)
    x2 = x.reshape(bsz * seq, D_MODEL)
    for layer in range(depth):
        if layer % 2 == 0:
            e = layer // 2
            z = _in_proj(x2, mod, norm_g, layer, w_in_ab[e].astype(BF16), ln_v_g[e], ln_v_b[e], seq)
            a = _attention(z, bsz, seq)
            b_s_wide = jnp.repeat(b_s[e], GROUP_DIM, axis=1)
            x2 = _out_proj(x2, mod, norm_g, layer, a, z, w_s[e], b_s_wide, w_out_ab[e].astype(BF16), seq)
        else:
            o = layer // 2
            x2 = _conformer(x2, mod, norm_g, layer, w_pw1[o].astype(BF16), b_pw1[o], w_dw[o], b_dw[o],
                            ln_c_g[o], ln_c_b[o], w_pw2[o].astype(BF16), b_pw2[o], seq)
        x2 = _ffn(x2, mod, norm_g, layer, w_gate_up[layer].astype(BF16), w_down[layer].astype(BF16), seq)
    return x2.reshape(bsz, seq, D_MODEL)
```

```python
import functools

import jax
import jax.numpy as jnp
from jax import lax
from jax.experimental import pallas as pl
from jax.experimental.pallas import tpu as pltpu

F32 = jnp.float32
BF16 = jnp.bfloat16

D_MODEL = 2048
N_HEADS = 8
HEAD_DIM = 128
WIDTH_A = N_HEADS * HEAD_DIM
N_GROUPS = 8
GROUP_DIM = 128
WIDTH_B = N_GROUPS * GROUP_DIM
CHUNK = 128
ATTN_BLOCK = 128
DILATIONS = (1, 4, 16)
AB_IN = 3 * WIDTH_A + 2 * WIDTH_B
CONV_WIDTH = 31
CONV_HALO = 32
D_FF = 5632
EPS = 1e-6
LANES = 128

VMEM_LIMIT_BYTES = 56 * 1024 * 1024
VMEM_LIMIT_FFN_BYTES = 60 * 1024 * 1024

TM_IN = 256
TM_OUT = 256
TM_CONF = 256
CONF_TN = 256
TM_FFN = 1024
TF_FFN = 512
TN_MOD = 1024
ROW_CHUNK = 16
ROW_UNROLL = 16
ATTN_UNROLL = 8


def _params(*sem):
    return pltpu.CompilerParams(dimension_semantics=sem, vmem_limit_bytes=VMEM_LIMIT_BYTES)


def _resident(shape):
    return pl.BlockSpec(shape, lambda *_: (0,) * len(shape), pipeline_mode=pl.Buffered(1))


def _mod_spec(layer, tpb):
    return pl.BlockSpec((None, None, 6, D_MODEL), lambda i, *_: (layer, i // tpb, 0, 0))


def _ng_spec(layer):
    return pl.BlockSpec((None, 4, D_MODEL), lambda *_: (layer, 0, 0))


def _layer_norm(x, g, b):
    mu = jnp.mean(x, axis=-1, keepdims=True)
    xc = x - mu
    var = jnp.mean(xc * xc, axis=-1, keepdims=True)
    return xc * lax.rsqrt(var + EPS) * g + b


def _modnorm_to(x_ref, ng_ref, g_row, mod_ref, shift_row, scale_row, h_ref, full_unroll=False):
    n_iter = x_ref.shape[0] // ROW_CHUNK

    def body(c, carry):
        rows = pl.ds(pl.multiple_of(c * ROW_CHUNK, ROW_CHUNK), ROW_CHUNK)
        x = x_ref[rows, :]
        rinv = lax.rsqrt(jnp.mean(x * x, axis=-1, keepdims=True) + EPS)
        gs = ng_ref[g_row:g_row + 1, :] * (1.0 + mod_ref[scale_row:scale_row + 1, :])
        h_ref[rows, :] = (x * rinv * gs + mod_ref[shift_row:shift_row + 1, :]).astype(BF16)
        return carry

    lax.fori_loop(0, n_iter, body, 0, unroll=n_iter if full_unroll else ROW_UNROLL)


def _residual_to(x_ref, y_ref, ng_ref, g_row, mod_ref, gate_row, out_ref, full_unroll=False):
    in_place = y_ref is out_ref
    chunk = ROW_CHUNK * ROW_UNROLL if in_place else ROW_CHUNK
    n_iter = x_ref.shape[0] // chunk
    unroll = n_iter if full_unroll else (1 if in_place else ROW_UNROLL)

    def body(c, carry):
        rows = pl.ds(pl.multiple_of(c * chunk, chunk), chunk)
        y = y_ref[rows, :]
        rinv = lax.rsqrt(jnp.mean(y * y, axis=-1, keepdims=True) + EPS)
        gg = ng_ref[g_row:g_row + 1, :] * mod_ref[gate_row:gate_row + 1, :]
        out_ref[rows, :] = x_ref[rows, :] + y * rinv * gg
        return carry

    lax.fori_loop(0, n_iter, body, 0, unroll=unroll)


def _mod_kernel(c_ref, w_ref, b_ref, o_ref):
    c_act = jax.nn.silu(c_ref[...]).astype(BF16)
    o_ref[...] = jnp.dot(c_act, w_ref[...].astype(BF16), preferred_element_type=F32) + b_ref[...]


def _modulation(c, w_mod, b_mod):
    bsz = c.shape[0]
    depth, _, n_out = w_mod.shape
    return pl.pallas_call(
        _mod_kernel,
        grid=(depth, n_out // TN_MOD),
        in_specs=[
            pl.BlockSpec((bsz, D_MODEL), lambda l, j: (0, 0)),
            pl.BlockSpec((None, D_MODEL, TN_MOD), lambda l, j: (l, 0, j)),
            pl.BlockSpec((None, 1, TN_MOD), lambda l, j: (l, 0, j)),
        ],
        out_specs=pl.BlockSpec((None, bsz, TN_MOD), lambda l, j: (l, 0, j)),
        out_shape=jax.ShapeDtypeStruct((depth, bsz, n_out), F32),
        compiler_params=_params("arbitrary", "arbitrary"),
        name="modulation",
    )(c, w_mod, b_mod.reshape(depth, 1, n_out))


def _in_proj_kernel(x_ref, mod_ref, ng_ref, w_ref, lng_ref, lnb_ref, z_ref, h_scr):
    _modnorm_to(x_ref, ng_ref, 0, mod_ref, 0, 1, h_scr)
    h = h_scr[...]
    for j in range(AB_IN // WIDTH_A):
        cols = slice(j * WIDTH_A, (j + 1) * WIDTH_A)
        z = jnp.dot(h, w_ref[:, cols], preferred_element_type=F32)
        if j >= 3:
            z = jax.nn.gelu(z)
        if j == 4:
            z = _layer_norm(z, lng_ref[...], lnb_ref[...])
        z_ref[:, cols] = z


def _in_proj(x2, mod, norm_g, layer, w_in, ln_g, ln_b, seq):
    rows = x2.shape[0]
    tm = TM_IN
    return pl.pallas_call(
        _in_proj_kernel,
        grid=(rows // tm,),
        in_specs=[
            pl.BlockSpec((tm, D_MODEL), lambda i: (i, 0)),
            _mod_spec(layer, seq // tm),
            _ng_spec(layer),
            _resident((D_MODEL, AB_IN)),
            pl.BlockSpec((1, WIDTH_B), lambda i: (0, 0)),
            pl.BlockSpec((1, WIDTH_B), lambda i: (0, 0)),
        ],
        out_specs=pl.BlockSpec((tm, AB_IN), lambda i: (i, 0)),
        out_shape=jax.ShapeDtypeStruct((rows, AB_IN), F32),
        scratch_shapes=[pltpu.VMEM((tm, D_MODEL), BF16)],
        compiler_params=_params("arbitrary"),
        name="in_proj",
    )(x2, mod, norm_g, w_in, ln_g.reshape(1, WIDTH_B), ln_b.reshape(1, WIDTH_B))


def _rows(start, size, stride):
    return pl.ds(start, size) if stride == 1 else pl.ds(start, size, stride=stride)


def _attn_kernel(q_ref, k_ref, v_ref, a_ref, k_pad, v_pad, o1, o2, o3, l1, l2, l3):
    seq = q_ref.shape[0]
    blk = ATTN_BLOCK
    pad = blk * max(DILATIONS)
    k_pad[0:pad, :] = jnp.zeros((pad, HEAD_DIM), F32)
    v_pad[0:pad, :] = jnp.zeros((pad, HEAD_DIM), F32)
    k_pad[pad:, :] = k_ref[...]
    v_pad[pad:, :] = v_ref[...]

    qi = lax.broadcasted_iota(jnp.int32, (blk, 2 * blk), 0)
    kj = lax.broadcasted_iota(jnp.int32, (blk, 2 * blk), 1)
    band = (kj >= qi) & (kj <= qi + blk)
    scale = HEAD_DIM ** -0.5

    for d, o_scr, l_scr in zip(DILATIONS, (o1, o2, o3), (l1, l2, l3)):
        per_class = seq // (blk * d)
        shift = per_class.bit_length() - 1

        def body(idx, carry, d=d, o_scr=o_scr, l_scr=l_scr, per_class=per_class, shift=shift):
            r = lax.shift_right_logical(idx, shift)
            m = lax.bitwise_and(idx, per_class - 1)
            start = r + m * (blk * d)
            q = q_ref[_rows(start, blk, d), :].astype(BF16)
            kk = k_pad[_rows(pad + start - blk * d, 2 * blk, d), :].astype(BF16)
            vv = v_pad[_rows(pad + start - blk * d, 2 * blk, d), :].astype(BF16)
            s = lax.dot_general(q, kk, (((1,), (1,)), ((), ())), preferred_element_type=F32) * scale
            first_key = jnp.where(m == 0, blk, 0)
            s = jnp.where(band & (kj >= first_key), s, -jnp.inf)
            mx = jnp.max(s, axis=-1, keepdims=True)
            p = jnp.exp(s - mx)
            den = jnp.sum(p, axis=-1, keepdims=True)
            o = jnp.dot(p.astype(BF16), vv, preferred_element_type=F32) / den
            o_scr[_rows(start, blk, d), :] = o
            l_scr[_rows(start, blk, d), :] = jnp.broadcast_to(mx + jnp.log(den), (blk, HEAD_DIM))
            return carry

        lax.fori_loop(0, seq // blk, body, 0, unroll=ATTN_UNROLL)

    def mix(c, carry):
        rows = pl.ds(pl.multiple_of(c * blk, blk), blk)
        la, lb, lc = l1[rows, :], l2[rows, :], l3[rows, :]
        lm = jnp.maximum(jnp.maximum(la, lb), lc)
        ea, eb, ec = jnp.exp(la - lm), jnp.exp(lb - lm), jnp.exp(lc - lm)
        esum = ea + eb + ec
        out = (ea / esum) * o1[rows, :] + (eb / esum) * o2[rows, :] + (ec / esum) * o3[rows, :]
        a_ref[rows, :] = out.astype(BF16)
        return carry

    lax.fori_loop(0, seq // blk, mix, 0)


def _attention(z, bsz, seq):
    z3 = z.reshape(bsz, seq, AB_IN)
    pad = ATTN_BLOCK * max(DILATIONS)
    head = lambda part: pl.BlockSpec((None, seq, HEAD_DIM), lambda b, h: (b, 0, part * N_HEADS + h))
    a = pl.pallas_call(
        _attn_kernel,
        grid=(bsz, N_HEADS),
        in_specs=[head(0), head(1), head(2)],
        out_specs=pl.BlockSpec((None, seq, HEAD_DIM), lambda b, h: (b, 0, h)),
        out_shape=jax.ShapeDtypeStruct((bsz, seq, WIDTH_A), BF16),
        scratch_shapes=[pltpu.VMEM((pad + seq, HEAD_DIM), F32)] * 2 + [pltpu.VMEM((seq, HEAD_DIM), F32)] * 6,
        compiler_params=_params("arbitrary", "arbitrary"),
        name="attention",
    )(z3, z3, z3)
    return a.reshape(bsz * seq, WIDTH_A)


def _out_proj_kernel(x_ref, mod_ref, ng_ref, a_ref, u_ref, g_ref, ws_ref, bs_ref, w_ref, out_ref,
                     cat_scr, y_scr):
    tm = x_ref.shape[0]
    n_chunk = tm // CHUNK
    cat_scr[:, 0:WIDTH_A] = a_ref[...]

    ci = lax.broadcasted_iota(jnp.int32, (CHUNK, CHUNK), 0)
    cj = lax.broadcasted_iota(jnp.int32, (CHUNK, CHUNK), 1)
    tril = cj <= ci
    for h in range(N_GROUPS):
        hs = slice(h * GROUP_DIM, (h + 1) * GROUP_DIM)
        w_c = jnp.where(tril, ws_ref[h], 0.0).astype(BF16)
        g_side = jnp.concatenate(
            [g_ref[c * CHUNK:(c + 1) * CHUNK, hs] for c in range(n_chunk)], axis=1).astype(BF16)
        sp = jnp.dot(w_c, g_side, preferred_element_type=F32)
        for c in range(n_chunk):
            rs = slice(c * CHUNK, (c + 1) * CHUNK)
            sp_c = sp[:, c * GROUP_DIM:(c + 1) * GROUP_DIM] + bs_ref[:, hs]
            cat_scr[rs, WIDTH_A + h * GROUP_DIM:WIDTH_A + (h + 1) * GROUP_DIM] = (
                u_ref[rs, hs] * sp_c).astype(BF16)

    y_scr[...] = jnp.dot(cat_scr[...], w_ref[...], preferred_element_type=F32)
    _residual_to(x_ref, y_scr, ng_ref, 1, mod_ref, 2, out_ref)


def _out_proj(x2, mod, norm_g, layer, a, z, w_s, b_s_wide, w_out, seq):
    rows = x2.shape[0]
    tm = TM_OUT
    return pl.pallas_call(
        _out_proj_kernel,
        grid=(rows // tm,),
        in_specs=[
            pl.BlockSpec((tm, D_MODEL), lambda i: (i, 0)),
            _mod_spec(layer, seq // tm),
            _ng_spec(layer),
            pl.BlockSpec((tm, WIDTH_A), lambda i: (i, 0)),
            pl.BlockSpec((tm, WIDTH_B), lambda i: (i, 3)),
            pl.BlockSpec((tm, WIDTH_B), lambda i: (i, 4)),
            pl.BlockSpec((N_GROUPS, CHUNK, CHUNK), lambda i: (0, 0, 0)),
            pl.BlockSpec((CHUNK, WIDTH_B), lambda i: (0, 0)),
            _resident((WIDTH_A + WIDTH_B, D_MODEL)),
        ],
        out_specs=pl.BlockSpec((tm, D_MODEL), lambda i: (i, 0)),
        out_shape=jax.ShapeDtypeStruct((rows, D_MODEL), F32),
        scratch_shapes=[pltpu.VMEM((tm, WIDTH_A + WIDTH_B), BF16), pltpu.VMEM((tm, D_MODEL), F32)],
        compiler_params=_params("arbitrary"),
        name="out_proj",
    )(x2, mod, norm_g, a, z, z, w_s, b_s_wide, w_out)


def _conformer_kernel(x_ref, mod_ref, ng_ref, w1_ref, b1_ref, wdw_ref, bdw_ref, lng_ref, lnb_ref,
                      w2_ref, b2_ref, out_ref, h_scr, glu_scr, conv_scr, y_scr, *, tiles_per_seq):
    tm = x_ref.shape[0]
    n_slab = D_MODEL // LANES

    @pl.when(pl.program_id(0) % tiles_per_seq == 0)
    def _():
        glu_scr[:, 0:CONV_HALO, :] = jnp.zeros((n_slab, CONV_HALO, LANES), F32)

    _modnorm_to(x_ref, ng_ref, 0, mod_ref, 0, 1, h_scr)
    base = CONV_HALO - (CONV_WIDTH - 1)
    slabs_per = CONF_TN // LANES
    n_col = D_MODEL // CONF_TN

    def glu_chunk(jj):
        a_cols = slice(jj * CONF_TN, (jj + 1) * CONF_TN)
        b_cols = slice(D_MODEL + jj * CONF_TN, D_MODEL + (jj + 1) * CONF_TN)
        h = h_scr[...]
        za = jnp.dot(h, w1_ref[:, a_cols], preferred_element_type=F32) + b1_ref[:, a_cols]
        zb = jnp.dot(h, w1_ref[:, b_cols], preferred_element_type=F32) + b1_ref[:, b_cols]
        glu = za * jax.nn.sigmoid(zb)
        for s in range(slabs_per):
            glu_scr[jj * slabs_per + s, CONV_HALO:, :] = glu[:, s * LANES:(s + 1) * LANES]

    def conv_chunk(jj):
        for s in range(slabs_per):
            j = jj * slabs_per + s
            cols = slice(j * LANES, (j + 1) * LANES)
            acc = jnp.broadcast_to(bdw_ref[:, cols], (tm, LANES))
            for k in range(CONV_WIDTH):
                acc = acc + wdw_ref[k:k + 1, cols] * glu_scr[j, pl.ds(base + k, tm), :]
            conv_scr[:, cols] = acc
            glu_scr[j, 0:CONV_HALO, :] = glu_scr[j, tm:tm + CONV_HALO, :]

    glu_chunk(0)
    for jj in range(1, n_col):
        conv_chunk(jj - 1)
        glu_chunk(jj)
    conv_chunk(n_col - 1)

    def ln_body(c, carry):
        rows = pl.ds(pl.multiple_of(c * ROW_CHUNK, ROW_CHUNK), ROW_CHUNK)
        a = jax.nn.silu(_layer_norm(conv_scr[rows, :], lng_ref[...], lnb_ref[...]))
        h_scr[rows, :] = a.astype(BF16)
        return carry

    lax.fori_loop(0, tm // ROW_CHUNK, ln_body, 0, unroll=ROW_UNROLL)
    y_scr[...] = jnp.dot(h_scr[...], w2_ref[...], preferred_element_type=F32) + b2_ref[...]
    _residual_to(x_ref, y_scr, ng_ref, 1, mod_ref, 2, out_ref)


def _conformer(x2, mod, norm_g, layer, w1, b1, w_dw, b_dw, ln_g, ln_b, w2, b2, seq):
    rows = x2.shape[0]
    tm = TM_CONF
    vec = lambda n: pl.BlockSpec((1, n), lambda i: (0, 0))
    return pl.pallas_call(
        functools.partial(_conformer_kernel, tiles_per_seq=seq // tm),
        grid=(rows // tm,),
        in_specs=[
            pl.BlockSpec((tm, D_MODEL), lambda i: (i, 0)),
            _mod_spec(layer, seq // tm),
            _ng_spec(layer),
            _resident((D_MODEL, 2 * D_MODEL)),
            vec(2 * D_MODEL),
            pl.BlockSpec((CONV_WIDTH, D_MODEL), lambda i: (0, 0)),
            vec(D_MODEL), vec(D_MODEL), vec(D_MODEL),
            _resident((D_MODEL, D_MODEL)),
            vec(D_MODEL),
        ],
        out_specs=pl.BlockSpec((tm, D_MODEL), lambda i: (i, 0)),
        out_shape=jax.ShapeDtypeStruct((rows, D_MODEL), F32),
        scratch_shapes=[
            pltpu.VMEM((tm, D_MODEL), BF16),
            pltpu.VMEM((D_MODEL // LANES, CONV_HALO + tm, LANES), F32),
            pltpu.VMEM((tm, D_MODEL), F32),
            pltpu.VMEM((tm, D_MODEL), F32),
        ],
        compiler_params=_params("arbitrary"),
        name="conformer",
    )(x2, mod, norm_g, w1, b1.reshape(1, -1), w_dw, b_dw.reshape(1, -1), ln_g.reshape(1, -1),
      ln_b.reshape(1, -1), w2, b2.reshape(1, -1))


def _ffn_kernel(x_ref, mod_ref, ng_ref, wg_ref, wu_ref, wd_ref, out_ref, h_scr):
    j = pl.program_id(1)
    last = pl.num_programs(1) - 1

    def hidden_tile(first):
        h = h_scr[...]
        gate = jnp.dot(h, wg_ref[...], preferred_element_type=F32)
        up = jnp.dot(h, wu_ref[...], preferred_element_type=F32)
        a = (jax.nn.silu(gate) * up).astype(BF16)
        y = jnp.dot(a, wd_ref[...], preferred_element_type=F32)
        if first:
            out_ref[...] = y
        else:
            out_ref[...] += y

    @pl.when(j == 0)
    def _():
        _modnorm_to(x_ref, ng_ref, 2, mod_ref, 3, 4, h_scr, full_unroll=True)
        hidden_tile(True)

    @pl.when((j > 0) & (j < last))
    def _():
        hidden_tile(False)

    @pl.when(j == last)
    def _():
        hidden_tile(False)
        _residual_to(x_ref, out_ref, ng_ref, 3, mod_ref, 5, out_ref, full_unroll=True)


def _ffn(x2, mod, norm_g, layer, w_gu, w_down, seq):
    rows = x2.shape[0]
    tm, tf = TM_FFN, TF_FFN
    n_f = D_FF // tf
    return pl.pallas_call(
        _ffn_kernel,
        grid=(rows // tm, n_f),
        in_specs=[
            pl.BlockSpec((tm, D_MODEL), lambda i, j: (i, 0)),
            _mod_spec(layer, seq // tm),
            _ng_spec(layer),
            pl.BlockSpec((D_MODEL, tf), lambda i, j: (0, j)),
            pl.BlockSpec((D_MODEL, tf), lambda i, j: (0, j + n_f)),
            pl.BlockSpec((tf, D_MODEL), lambda i, j: (j, 0)),
        ],
        out_specs=pl.BlockSpec((tm, D_MODEL), lambda i, j: (i, 0)),
        out_shape=jax.ShapeDtypeStruct((rows, D_MODEL), F32),
        scratch_shapes=[pltpu.VMEM((tm, D_MODEL), BF16)],
        compiler_params=pltpu.CompilerParams(dimension_semantics=("arbitrary", "arbitrary"),
                                             vmem_limit_bytes=VMEM_LIMIT_FFN_BYTES),
        name="ffn",
    )(x2, mod, norm_g, w_gu, w_gu, w_down)


def kernel(x, c, w_mod, b_mod, norm_g, w_in_ab, w_s, b_s, ln_v_g, ln_v_b, w_out_ab, w_pw1, b_pw1, w_dw,
           b_dw, ln_c_g, ln_c_b, w_pw2, b_pw2, w_gate_up, w_down):
    bsz, seq, _ = x.shape
    depth = w_mod.shape[0]
    mod = _modulation(c, w_mod, b_mod).reshape(depth, bsz, 6, D_MODEL)
    x2 = x.reshape(bsz * seq, D_MODEL)
    for layer in range(depth):
        if layer % 2 == 0:
            e = layer // 2
            z = _in_proj(x2, mod, norm_g, layer, w_in_ab[e].astype(BF16), ln_v_g[e], ln_v_b[e], seq)
            a = _attention(z, bsz, seq)
            b_s_wide = jnp.repeat(b_s[e], GROUP_DIM, axis=1)
            x2 = _out_proj(x2, mod, norm_g, layer, a, z, w_s[e], b_s_wide, w_out_ab[e].astype(BF16), seq)
        else:
            o = layer // 2
            x2 = _conformer(x2, mod, norm_g, layer, w_pw1[o].astype(BF16), b_pw1[o], w_dw[o], b_dw[o],
                            ln_c_g[o], ln_c_b[o], w_pw2[o].astype(BF16), b_pw2[o], seq)
        x2 = _ffn(x2, mod, norm_g, layer, w_gate_up[layer].astype(BF16), w_down[layer].astype(BF16), seq)
    return x2.reshape(bsz, seq, D_MODEL)
```

```python
import functools

import jax
import jax.numpy as jnp
from jax import lax
from jax.experimental import pallas as pl
from jax.experimental.pallas import tpu as pltpu

F32 = jnp.float32
BF16 = jnp.bfloat16

D_MODEL = 2048
N_HEADS = 8
HEAD_DIM = 128
WIDTH_A = N_HEADS * HEAD_DIM
N_GROUPS = 8
GROUP_DIM = 128
WIDTH_B = N_GROUPS * GROUP_DIM
CHUNK = 128
ATTN_BLOCK = 128
DILATIONS = (1, 4, 16)
AB_IN = 3 * WIDTH_A + 2 * WIDTH_B
CONV_WIDTH = 31
CONV_HALO = 32
D_FF = 5632
EPS = 1e-6
LANES = 128

VMEM_LIMIT_BYTES = 56 * 1024 * 1024
VMEM_LIMIT_FFN_BYTES = 60 * 1024 * 1024

TM_IN = 256
TM_OUT = 256
TM_CONF = 256
CONF_TN = 256
TM_FFN = 1024
TF_FFN = 512
TN_MOD = 1024
ROW_CHUNK = 16
ROW_UNROLL = 16
ATTN_UNROLL = 8


def _params(*sem):
    return pltpu.CompilerParams(dimension_semantics=sem, vmem_limit_bytes=VMEM_LIMIT_BYTES)


def _resident(shape):
    return pl.BlockSpec(shape, lambda *_: (0,) * len(shape), pipeline_mode=pl.Buffered(1))


def _mod_spec(layer, tpb):
    return pl.BlockSpec((None, None, 6, D_MODEL), lambda i, *_: (layer, i // tpb, 0, 0))


def _ng_spec(layer):
    return pl.BlockSpec((None, 4, D_MODEL), lambda *_: (layer, 0, 0))


def _layer_norm(x, g, b):
    mu = jnp.mean(x, axis=-1, keepdims=True)
    xc = x - mu
    var = jnp.mean(xc * xc, axis=-1, keepdims=True)
    return xc * lax.rsqrt(var + EPS) * g + b


def _modnorm_to(x_ref, ng_ref, g_row, mod_ref, shift_row, scale_row, h_ref, full_unroll=False):
    n_iter = x_ref.shape[0] // ROW_CHUNK

    def body(c, carry):
        rows = pl.ds(pl.multiple_of(c * ROW_CHUNK, ROW_CHUNK), ROW_CHUNK)
        x = x_ref[rows, :]
        rinv = lax.rsqrt(jnp.mean(x * x, axis=-1, keepdims=True) + EPS)
        gs = ng_ref[g_row:g_row + 1, :] * (1.0 + mod_ref[scale_row:scale_row + 1, :])
        h_ref[rows, :] = (x * rinv * gs + mod_ref[shift_row:shift_row + 1, :]).astype(BF16)
        return carry

    lax.fori_loop(0, n_iter, body, 0, unroll=n_iter if full_unroll else ROW_UNROLL)


def _residual_to(x_ref, y_ref, ng_ref, g_row, mod_ref, gate_row, out_ref, full_unroll=False):
    in_place = y_ref is out_ref
    chunk = ROW_CHUNK * ROW_UNROLL if in_place else ROW_CHUNK
    n_iter = x_ref.shape[0] // chunk
    unroll = n_iter if full_unroll else (1 if in_place else ROW_UNROLL)

    def body(c, carry):
        rows = pl.ds(pl.multiple_of(c * chunk, chunk), chunk)
        y = y_ref[rows, :]
        rinv = lax.rsqrt(jnp.mean(y * y, axis=-1, keepdims=True) + EPS)
        gg = ng_ref[g_row:g_row + 1, :] * mod_ref[gate_row:gate_row + 1, :]
        out_ref[rows, :] = x_ref[rows, :] + y * rinv * gg
        return carry

    lax.fori_loop(0, n_iter, body, 0, unroll=unroll)


def _mod_kernel(c_ref, w_ref, b_ref, o_ref):
    c_act = jax.nn.silu(c_ref[...]).astype(BF16)
    o_ref[...] = jnp.dot(c_act, w_ref[...].astype(BF16), preferred_element_type=F32) + b_ref[...]


def _modulation(c, w_mod, b_mod):
    bsz = c.shape[0]
    depth, _, n_out = w_mod.shape
    return pl.pallas_call(
        _mod_kernel,
        grid=(depth, n_out // TN_MOD),
        in_specs=[
            pl.BlockSpec((bsz, D_MODEL), lambda l, j: (0, 0)),
            pl.BlockSpec((None, D_MODEL, TN_MOD), lambda l, j: (l, 0, j)),
            pl.BlockSpec((None, 1, TN_MOD), lambda l, j: (l, 0, j)),
        ],
        out_specs=pl.BlockSpec((None, bsz, TN_MOD), lambda l, j: (l, 0, j)),
        out_shape=jax.ShapeDtypeStruct((depth, bsz, n_out), F32),
        compiler_params=_params("arbitrary", "arbitrary"),
        name="modulation",
    )(c, w_mod, b_mod.reshape(depth, 1, n_out))


def _in_proj_kernel(x_ref, mod_ref, ng_ref, w_ref, lng_ref, lnb_ref, z_ref, h_scr):
    _modnorm_to(x_ref, ng_ref, 0, mod_ref, 0, 1, h_scr)
    h = h_scr[...]
    for j in range(AB_IN // WIDTH_A):
        cols = slice(j * WIDTH_A, (j + 1) * WIDTH_A)
        z = jnp.dot(h, w_ref[:, cols], preferred_element_type=F32)
        if j >= 3:
            z = jax.nn.gelu(z)
        if j == 4:
            z = _layer_norm(z, lng_ref[...], lnb_ref[...])
        z_ref[:, cols] = z


def _in_proj(x2, mod, norm_g, layer, w_in, ln_g, ln_b, seq):
    rows = x2.shape[0]
    tm = TM_IN
    return pl.pallas_call(
        _in_proj_kernel,
        grid=(rows // tm,),
        in_specs=[
            pl.BlockSpec((tm, D_MODEL), lambda i: (i, 0)),
            _mod_spec(layer, seq // tm),
            _ng_spec(layer),
            _resident((D_MODEL, AB_IN)),
            pl.BlockSpec((1, WIDTH_B), lambda i: (0, 0)),
            pl.BlockSpec((1, WIDTH_B), lambda i: (0, 0)),
        ],
        out_specs=pl.BlockSpec((tm, AB_IN), lambda i: (i, 0)),
        out_shape=jax.ShapeDtypeStruct((rows, AB_IN), F32),
        scratch_shapes=[pltpu.VMEM((tm, D_MODEL), BF16)],
        compiler_params=_params("arbitrary"),
        name="in_proj",
    )(x2, mod, norm_g, w_in, ln_g.reshape(1, WIDTH_B), ln_b.reshape(1, WIDTH_B))


def _rows(start, size, stride):
    return pl.ds(start, size) if stride == 1 else pl.ds(start, size, stride=stride)


def _attn_kernel(q_ref, k_ref, v_ref, a_ref, k_pad, v_pad, o1, o2, o3, l1, l2, l3, k_cls, v_cls):
    seq = q_ref.shape[0]
    blk = ATTN_BLOCK
    pad = blk * max(DILATIONS)
    k_pad[0:pad, :] = jnp.zeros((pad, HEAD_DIM), F32)
    v_pad[0:pad, :] = jnp.zeros((pad, HEAD_DIM), F32)
    k_pad[pad:, :] = k_ref[...]
    v_pad[pad:, :] = v_ref[...]

    d_max = max(DILATIONS)
    cls = seq // d_max + blk
    for r in range(d_max):
        k_cls[r * cls:r * cls + blk, :] = jnp.zeros((blk, HEAD_DIM), BF16)
        v_cls[r * cls:r * cls + blk, :] = jnp.zeros((blk, HEAD_DIM), BF16)
        k_cls[r * cls + blk:(r + 1) * cls, :] = k_ref[pl.ds(r, seq // d_max, stride=d_max), :].astype(BF16)
        v_cls[r * cls + blk:(r + 1) * cls, :] = v_ref[pl.ds(r, seq // d_max, stride=d_max), :].astype(BF16)

    qi = lax.broadcasted_iota(jnp.int32, (blk, 2 * blk), 0)
    kj = lax.broadcasted_iota(jnp.int32, (blk, 2 * blk), 1)
    band = (kj >= qi) & (kj <= qi + blk)
    scale = HEAD_DIM ** -0.5

    for d, o_scr, l_scr in zip(DILATIONS, (o1, o2, o3), (l1, l2, l3)):
        per_class = seq // (blk * d)
        shift = per_class.bit_length() - 1

        def body(idx, carry, d=d, o_scr=o_scr, l_scr=l_scr, per_class=per_class, shift=shift):
            r = lax.shift_right_logical(idx, shift)
            m = lax.bitwise_and(idx, per_class - 1)
            start = r + m * (blk * d)
            q = q_ref[_rows(start, blk, d), :].astype(BF16)
            if d == d_max:
                k0 = pl.multiple_of(r * cls + m * blk, blk)
                kk = k_cls[pl.ds(k0, 2 * blk), :]
                vv = v_cls[pl.ds(k0, 2 * blk), :]
            else:
                kk = k_pad[_rows(pad + start - blk * d, 2 * blk, d), :].astype(BF16)
                vv = v_pad[_rows(pad + start - blk * d, 2 * blk, d), :].astype(BF16)
            s = lax.dot_general(q, kk, (((1,), (1,)), ((), ())), preferred_element_type=F32) * scale
            first_key = jnp.where(m == 0, blk, 0)
            s = jnp.where(band & (kj >= first_key), s, -jnp.inf)
            mx = jnp.max(s, axis=-1, keepdims=True)
            p = jnp.exp(s - mx)
            den = jnp.sum(p, axis=-1, keepdims=True)
            o = jnp.dot(p.astype(BF16), vv, preferred_element_type=F32) / den
            o_scr[_rows(start, blk, d), :] = o
            l_scr[_rows(start, blk, d), :] = jnp.broadcast_to(mx + jnp.log(den), (blk, HEAD_DIM))
            return carry

        lax.fori_loop(0, seq // blk, body, 0, unroll=ATTN_UNROLL)

    def mix(c, carry):
        rows = pl.ds(pl.multiple_of(c * blk, blk), blk)
        la, lb, lc = l1[rows, :], l2[rows, :], l3[rows, :]
        lm = jnp.maximum(jnp.maximum(la, lb), lc)
        ea, eb, ec = jnp.exp(la - lm), jnp.exp(lb - lm), jnp.exp(lc - lm)
        esum = ea + eb + ec
        out = (ea / esum) * o1[rows, :] + (eb / esum) * o2[rows, :] + (ec / esum) * o3[rows, :]
        a_ref[rows, :] = out.astype(BF16)
        return carry

    lax.fori_loop(0, seq // blk, mix, 0)


def _attention(z, bsz, seq):
    z3 = z.reshape(bsz, seq, AB_IN)
    pad = ATTN_BLOCK * max(DILATIONS)
    head = lambda part: pl.BlockSpec((None, seq, HEAD_DIM), lambda b, h: (b, 0, part * N_HEADS + h))
    a = pl.pallas_call(
        _attn_kernel,
        grid=(bsz, N_HEADS),
        in_specs=[head(0), head(1), head(2)],
        out_specs=pl.BlockSpec((None, seq, HEAD_DIM), lambda b, h: (b, 0, h)),
        out_shape=jax.ShapeDtypeStruct((bsz, seq, WIDTH_A), BF16),
        scratch_shapes=([pltpu.VMEM((pad + seq, HEAD_DIM), F32)] * 2 + [pltpu.VMEM((seq, HEAD_DIM), F32)] * 6
                        + [pltpu.VMEM((pad + seq, HEAD_DIM), BF16)] * 2),
        compiler_params=_params("arbitrary", "arbitrary"),
        name="attention",
    )(z3, z3, z3)
    return a.reshape(bsz * seq, WIDTH_A)


def _out_proj_kernel(x_ref, mod_ref, ng_ref, a_ref, u_ref, g_ref, ws_ref, bs_ref, w_ref, out_ref,
                     cat_scr, y_scr):
    tm = x_ref.shape[0]
    n_chunk = tm // CHUNK
    cat_scr[:, 0:WIDTH_A] = a_ref[...]

    ci = lax.broadcasted_iota(jnp.int32, (CHUNK, CHUNK), 0)
    cj = lax.broadcasted_iota(jnp.int32, (CHUNK, CHUNK), 1)
    tril = cj <= ci
    for h in range(N_GROUPS):
        hs = slice(h * GROUP_DIM, (h + 1) * GROUP_DIM)
        w_c = jnp.where(tril, ws_ref[h], 0.0).astype(BF16)
        g_side = jnp.concatenate(
            [g_ref[c * CHUNK:(c + 1) * CHUNK, hs] for c in range(n_chunk)], axis=1).astype(BF16)
        sp = jnp.dot(w_c, g_side, preferred_element_type=F32)
        for c in range(n_chunk):
            rs = slice(c * CHUNK, (c + 1) * CHUNK)
            sp_c = sp[:, c * GROUP_DIM:(c + 1) * GROUP_DIM] + bs_ref[:, hs]
            cat_scr[rs, WIDTH_A + h * GROUP_DIM:WIDTH_A + (h + 1) * GROUP_DIM] = (
                u_ref[rs, hs] * sp_c).astype(BF16)

    y_scr[...] = jnp.dot(cat_scr[...], w_ref[...], preferred_element_type=F32)
    _residual_to(x_ref, y_scr, ng_ref, 1, mod_ref, 2, out_ref)


def _out_proj(x2, mod, norm_g, layer, a, z, w_s, b_s_wide, w_out, seq):
    rows = x2.shape[0]
    tm = TM_OUT
    return pl.pallas_call(
        _out_proj_kernel,
        grid=(rows // tm,),
        in_specs=[
            pl.BlockSpec((tm, D_MODEL), lambda i: (i, 0)),
            _mod_spec(layer, seq // tm),
            _ng_spec(layer),
            pl.BlockSpec((tm, WIDTH_A), lambda i: (i, 0)),
            pl.BlockSpec((tm, WIDTH_B), lambda i: (i, 3)),
            pl.BlockSpec((tm, WIDTH_B), lambda i: (i, 4)),
            pl.BlockSpec((N_GROUPS, CHUNK, CHUNK), lambda i: (0, 0, 0)),
            pl.BlockSpec((CHUNK, WIDTH_B), lambda i: (0, 0)),
            _resident((WIDTH_A + WIDTH_B, D_MODEL)),
        ],
        out_specs=pl.BlockSpec((tm, D_MODEL), lambda i: (i, 0)),
        out_shape=jax.ShapeDtypeStruct((rows, D_MODEL), F32),
        scratch_shapes=[pltpu.VMEM((tm, WIDTH_A + WIDTH_B), BF16), pltpu.VMEM((tm, D_MODEL), F32)],
        compiler_params=_params("arbitrary"),
        name="out_proj",
    )(x2, mod, norm_g, a, z, z, w_s, b_s_wide, w_out)


def _conformer_kernel(x_ref, mod_ref, ng_ref, w1_ref, b1_ref, wdw_ref, bdw_ref, lng_ref, lnb_ref,
                      w2_ref, b2_ref, out_ref, h_scr, glu_scr, conv_scr, y_scr, *, tiles_per_seq):
    tm = x_ref.shape[0]
    n_slab = D_MODEL // LANES

    @pl.when(pl.program_id(0) % tiles_per_seq == 0)
    def _():
        glu_scr[:, 0:CONV_HALO, :] = jnp.zeros((n_slab, CONV_HALO, LANES), F32)

    _modnorm_to(x_ref, ng_ref, 0, mod_ref, 0, 1, h_scr)
    base = CONV_HALO - (CONV_WIDTH - 1)
    slabs_per = CONF_TN // LANES
    n_col = D_MODEL // CONF_TN

    def glu_chunk(jj):
        a_cols = slice(jj * CONF_TN, (jj + 1) * CONF_TN)
        b_cols = slice(D_MODEL + jj * CONF_TN, D_MODEL + (jj + 1) * CONF_TN)
        h = h_scr[...]
        za = jnp.dot(h, w1_ref[:, a_cols], preferred_element_type=F32) + b1_ref[:, a_cols]
        zb = jnp.dot(h, w1_ref[:, b_cols], preferred_element_type=F32) + b1_ref[:, b_cols]
        glu = za * jax.nn.sigmoid(zb)
        for s in range(slabs_per):
            glu_scr[jj * slabs_per + s, CONV_HALO:, :] = glu[:, s * LANES:(s + 1) * LANES]

    def conv_chunk(jj):
        for s in range(slabs_per):
            j = jj * slabs_per + s
            cols = slice(j * LANES, (j + 1) * LANES)
            acc = jnp.broadcast_to(bdw_ref[:, cols], (tm, LANES))
            for k in range(CONV_WIDTH):
                acc = acc + wdw_ref[k:k + 1, cols] * glu_scr[j, pl.ds(base + k, tm), :]
            conv_scr[:, cols] = acc
            glu_scr[j, 0:CONV_HALO, :] = glu_scr[j, tm:tm + CONV_HALO, :]

    glu_chunk(0)
    for jj in range(1, n_col):
        conv_chunk(jj - 1)
        glu_chunk(jj)
    conv_chunk(n_col - 1)

    def ln_body(c, carry):
        rows = pl.ds(pl.multiple_of(c * ROW_CHUNK, ROW_CHUNK), ROW_CHUNK)
        a = jax.nn.silu(_layer_norm(conv_scr[rows, :], lng_ref[...], lnb_ref[...]))
        h_scr[rows, :] = a.astype(BF16)
        return carry

    lax.fori_loop(0, tm // ROW_CHUNK, ln_body, 0, unroll=ROW_UNROLL)
    y_scr[...] = jnp.dot(h_scr[...], w2_ref[...], preferred_element_type=F32) + b2_ref[...]
    _residual_to(x_ref, y_scr, ng_ref, 1, mod_ref, 2, out_ref)


def _conformer(x2, mod, norm_g, layer, w1, b1, w_dw, b_dw, ln_g, ln_b, w2, b2, seq):
    rows = x2.shape[0]
    tm = TM_CONF
    vec = lambda n: pl.BlockSpec((1, n), lambda i: (0, 0))
    return pl.pallas_call(
        functools.partial(_conformer_kernel, tiles_per_seq=seq // tm),
        grid=(rows // tm,),
        in_specs=[
            pl.BlockSpec((tm, D_MODEL), lambda i: (i, 0)),
            _mod_spec(layer, seq // tm),
            _ng_spec(layer),
            _resident((D_MODEL, 2 * D_MODEL)),
            vec(2 * D_MODEL),
            pl.BlockSpec((CONV_WIDTH, D_MODEL), lambda i: (0, 0)),
            vec(D_MODEL), vec(D_MODEL), vec(D_MODEL),
            _resident((D_MODEL, D_MODEL)),
            vec(D_MODEL),
        ],
        out_specs=pl.BlockSpec((tm, D_MODEL), lambda i: (i, 0)),
        out_shape=jax.ShapeDtypeStruct((rows, D_MODEL), F32),
        scratch_shapes=[
            pltpu.VMEM((tm, D_MODEL), BF16),
            pltpu.VMEM((D_MODEL // LANES, CONV_HALO + tm, LANES), F32),
            pltpu.VMEM((tm, D_MODEL), F32),
            pltpu.VMEM((tm, D_MODEL), F32),
        ],
        compiler_params=_params("arbitrary"),
        name="conformer",
    )(x2, mod, norm_g, w1, b1.reshape(1, -1), w_dw, b_dw.reshape(1, -1), ln_g.reshape(1, -1),
      ln_b.reshape(1, -1), w2, b2.reshape(1, -1))


def _ffn_kernel(x_ref, mod_ref, ng_ref, wg_ref, wu_ref, wd_ref, out_ref, h_scr):
    j = pl.program_id(1)
    last = pl.num_programs(1) - 1

    def hidden_tile(first):
        h = h_scr[...]
        gate = jnp.dot(h, wg_ref[...], preferred_element_type=F32)
        up = jnp.dot(h, wu_ref[...], preferred_element_type=F32)
        a = (jax.nn.silu(gate) * up).astype(BF16)
        y = jnp.dot(a, wd_ref[...], preferred_element_type=F32)
        if first:
            out_ref[...] = y
        else:
            out_ref[...] += y

    @pl.when(j == 0)
    def _():
        _modnorm_to(x_ref, ng_ref, 2, mod_ref, 3, 4, h_scr, full_unroll=True)
        hidden_tile(True)

    @pl.when((j > 0) & (j < last))
    def _():
        hidden_tile(False)

    @pl.when(j == last)
    def _():
        hidden_tile(False)
        _residual_to(x_ref, out_ref, ng_ref, 3, mod_ref, 5, out_ref, full_unroll=True)


def _ffn(x2, mod, norm_g, layer, w_gu, w_down, seq):
    rows = x2.shape[0]
    tm, tf = TM_FFN, TF_FFN
    n_f = D_FF // tf
    return pl.pallas_call(
        _ffn_kernel,
        grid=(rows // tm, n_f),
        in_specs=[
            pl.BlockSpec((tm, D_MODEL), lambda i, j: (i, 0)),
            _mod_spec(layer, seq // tm),
            _ng_spec(layer),
            pl.BlockSpec((D_MODEL, tf), lambda i, j: (0, j)),
            pl.BlockSpec((D_MODEL, tf), lambda i, j: (0, j + n_f)),
            pl.BlockSpec((tf, D_MODEL), lambda i, j: (j, 0)),
        ],
        out_specs=pl.BlockSpec((tm, D_MODEL), lambda i, j: (i, 0)),
        out_shape=jax.ShapeDtypeStruct((rows, D_MODEL), F32),
        scratch_shapes=[pltpu.VMEM((tm, D_MODEL), BF16)],
        compiler_params=pltpu.CompilerParams(dimension_semantics=("arbitrary", "arbitrary"),
                                             vmem_limit_bytes=VMEM_LIMIT_FFN_BYTES),
        name="ffn",
    )(x2, mod, norm_g, w_gu, w_gu, w_down)


def kernel(x, c, w_mod, b_mod, norm_g, w_in_ab, w_s, b_s, ln_v_g, ln_v_b, w_out_ab, w_pw1, b_pw1, w_dw,
           b_dw, ln_c_g, ln_c_b, w_pw2, b_pw2, w_gate_up, w_down):
    bsz, seq, _ = x.shape
    depth = w_mod.shape[0]
    mod = _modulation(c, w_mod, b_mod).reshape(depth, bsz, 6, D_MODEL)
    x2 = x.reshape(bsz * seq, D_MODEL)
    for layer in range(depth):
        if layer % 2 == 0:
            e = layer // 2
            z = _in_proj(x2, mod, norm_g, layer, w_in_ab[e].astype(BF16), ln_v_g[e], ln_v_b[e], seq)
            a = _attention(z, bsz, seq)
            b_s_wide = jnp.repeat(b_s[e], GROUP_DIM, axis=1)
            x2 = _out_proj(x2, mod, norm_g, layer, a, z, w_s[e], b_s_wide, w_out_ab[e].astype(BF16), seq)
        else:
            o = layer // 2
            x2 = _conformer(x2, mod, norm_g, layer, w_pw1[o].astype(BF16), b_pw1[o], w_dw[o], b_dw[o],
                            ln_c_g[o], ln_c_b[o], w_pw2[o].astype(BF16), b_pw2[o], seq)
        x2 = _ffn(x2, mod, norm_g, layer, w_gate_up[layer].astype(BF16), w_down[layer].astype(BF16), seq)
    return x2.reshape(bsz, seq, D_MODEL)
```
